```python
import jax, jax.numpy as jnp
from jax import lax
import numpy as np

D_MODEL = 2048
BATCH = 4
SEQ = 2048
DEPTH = 1
DEC_BATCH = 128
DEC_SEQ = 1
PAST_LEN = 8192
PAGE_SIZE = 128

MLA_V_DIM = 128
FOX_HEAD_DIM = 128
H_MLA = (D_MODEL // 2) // MLA_V_DIM
H_FOX = (D_MODEL // 2) // FOX_HEAD_DIM
MLA_NOPE_DIM = 128
MLA_ROPE_DIM = 64
MLA_Q_RANK = 512
MLA_KV_RANK = 256
MLA_WIDTH = H_MLA * MLA_V_DIM
FOX_WIDTH = H_FOX * FOX_HEAD_DIM
IN_WIDTH = MLA_Q_RANK + MLA_KV_RANK + MLA_ROPE_DIM + 3 * FOX_WIDTH + H_FOX
MLA_SCALE = (MLA_NOPE_DIM + MLA_ROPE_DIM) ** -0.5
FOX_SCALE = FOX_HEAD_DIM ** -0.5
ROPE_THETA = 10000.0
Q_BLOCK = 128
FORGET_BIAS = 4.0
N_EXPERTS = 64
N_GROUPS = 8
TOPK_GROUPS = 4
TOP_K = 8
EXPERT_FF = 512
SHARED_FF = 512
ROUTED_SCALE = 2.5
EXPERT_BLOCK = 128
NORM_EPS = 1e-6
NEG_BIG = -1e30

kernel_name = "hymba_mla_fox_moe_adaln_step"


def rmsnorm(x, g):
    xf = x.astype(jnp.float32)
    y = xf * lax.rsqrt(jnp.mean(xf * xf, axis=-1, keepdims=True) + NORM_EPS)
    return (y * g.astype(jnp.float32)).astype(x.dtype)


def adaln(x, g, shift, scale):
    return rmsnorm(x, g) * (1 + scale[:, None, :]) + shift[:, None, :]


def modulation(c, w_ada, b_ada):
    mod = jax.nn.silu(c) @ w_ada + b_ada
    return jnp.split(mod, 6, axis=-1)


def apply_rope(x, pos):
    half = x.shape[-1] // 2
    inv = ROPE_THETA ** (-jnp.arange(half, dtype=jnp.float32) / half)
    ang = pos.astype(jnp.float32)[:, None] * inv[None, :]
    ang = ang.reshape((ang.shape[0],) + (1,) * (x.ndim - 3) + (half,))
    cos, sin = jnp.cos(ang).astype(x.dtype), jnp.sin(ang).astype(x.dtype)
    x1, x2 = x[..., :half], x[..., half:]
    return jnp.concatenate([x1 * cos - x2 * sin, x1 * sin + x2 * cos], axis=-1)


def project(h, pos, w_in, g_q_a, w_q_b, g_q_mla, g_ckv, g_kpe, g_q_fox, g_k_fox, b_forget):
    B, S, _ = h.shape
    cuts = [int(v) for v in np.cumsum([MLA_Q_RANK, MLA_KV_RANK, MLA_ROPE_DIM, FOX_WIDTH, FOX_WIDTH, FOX_WIDTH])]
    q_a, ckv, kpe, qf, kf, vf, f_logit = jnp.split(h @ w_in, cuts, axis=-1)
    q = (rmsnorm(q_a, g_q_a) @ w_q_b).reshape(B, S, H_MLA, MLA_NOPE_DIM + MLA_ROPE_DIM)
    q = rmsnorm(q, g_q_mla)
    q_nope = q[..., :MLA_NOPE_DIM]
    q_pe = apply_rope(q[..., MLA_NOPE_DIM:], pos)
    ckv = rmsnorm(ckv, g_ckv)
    kpe = apply_rope(rmsnorm(kpe, g_kpe), pos)
    qf = rmsnorm(qf.reshape(B, S, H_FOX, FOX_HEAD_DIM), g_q_fox)
    kf = rmsnorm(kf.reshape(B, S, H_FOX, FOX_HEAD_DIM), g_k_fox)
    vf = vf.reshape(B, S, H_FOX, FOX_HEAD_DIM)
    logf = jax.nn.log_sigmoid(f_logit.astype(jnp.float32) + b_forget.astype(jnp.float32))
    return q_nope, q_pe, ckv, kpe, qf, kf, vf, logf


def causal_block_attention(q, k, v, cum_logf):
    B, S, H, Dk = q.shape
    Dv = v.shape[-1]
    nb = S // Q_BLOCK
    scale = Dk ** -0.5
    kpos = jnp.arange(S)
    f_k = None if cum_logf is None else cum_logf.transpose(0, 2, 1)

    def one_block(i):
        start = i * Q_BLOCK
        qb = lax.dynamic_slice_in_dim(q, start, Q_BLOCK, axis=1)
        logits = jnp.einsum('bqhd,bkhd->bhqk', qb, k).astype(jnp.float32) * scale
        if f_k is not None:
            f_q = lax.dynamic_slice_in_dim(f_k, start, Q_BLOCK, axis=2)
            logits = logits + f_q[..., :, None] - f_k[:, :, None, :]
        qpos = start + jnp.arange(Q_BLOCK)
        logits = jnp.where(kpos[None, :] <= qpos[:, None], logits, NEG_BIG)
        p = jax.nn.softmax(logits, axis=-1)
        return jnp.einsum('bhqk,bkhd->bqhd', p.astype(v.dtype), v)

    out = lax.map(one_block, jnp.arange(nb))
    return out.transpose(1, 0, 2, 3, 4).reshape(B, S, H, Dv)


def prompt_mixers(q_nope, q_pe, ckv, kpe, qf, kf, vf, logf, w_uk, w_uv):
    B, S = q_nope.shape[:2]
    k_nope = jnp.einsum('bsr,rhd->bshd', ckv, w_uk)
    v_mla = jnp.einsum('bsr,rhd->bshd', ckv, w_uv)
    q_mla = jnp.concatenate([q_nope, q_pe], axis=-1)
    k_mla = jnp.concatenate([k_nope, jnp.broadcast_to(kpe[:, :, None, :], (B, S, H_MLA, MLA_ROPE_DIM))], axis=-1)
    o_mla = causal_block_attention(q_mla, k_mla, v_mla, None)
    o_fox = causal_block_attention(qf, kf, vf, lax.cumsum(logf, axis=1))
    return jnp.concatenate([o_mla.reshape(B, S, MLA_WIDTH), o_fox.reshape(B, S, FOX_WIDTH)], axis=-1)


def online_step(state, logits, values, eq):
    m, l, acc = state
    m_new = jnp.maximum(m, logits.max(-1))
    corr = jnp.exp(m - m_new)
    p = jnp.exp(logits - m_new[..., None])
    l = l * corr + p.sum(-1)
    acc = acc * corr[..., None] + jnp.einsum(eq, p.astype(values.dtype), values).astype(jnp.float32)
    return (m_new, l, acc)


def sample_mixers(q_nope, q_pe, ckv, kpe, qf, kf, vf, logf,
                  cache_ckv, cache_kpe, cache_fk, cache_fv, cache_flogf, page_table, w_uk, w_uv):
    Bd, T = q_nope.shape[:2]
    f32 = jnp.float32
    q_lat = jnp.einsum('bthd,rhd->bhtr', q_nope, w_uk)
    q_pe_h = q_pe.transpose(0, 2, 1, 3)
    qf_h = qf.transpose(0, 2, 1, 3)
    f_new = lax.cumsum(logf, axis=1).transpose(0, 2, 1)
    causal = jnp.arange(T)[:, None] >= jnp.arange(T)[None, :]

    def mla_logits(ckv_rows, kpe_rows):
        return (jnp.einsum('bhtr,bsr->bhts', q_lat, ckv_rows)
                + jnp.einsum('bhtd,bsd->bhts', q_pe_h, kpe_rows)).astype(f32) * MLA_SCALE

    def fox_logits(k_rows, bias_rows):
        return (jnp.einsum('bhtd,bshd->bhts', qf_h, k_rows).astype(f32) * FOX_SCALE
                + f_new[..., :, None] + bias_rows[:, :, None, :])

    def init(h, dv):
        return (jnp.full((Bd, h, T), NEG_BIG, f32), jnp.zeros((Bd, h, T), f32), jnp.zeros((Bd, h, T, dv), f32))

    st_mla = online_step(init(H_MLA, MLA_KV_RANK), jnp.where(causal, mla_logits(ckv, kpe), NEG_BIG), ckv, 'bhts,bsr->bhtr')
    st_fox = online_step(init(H_FOX, FOX_HEAD_DIM), jnp.where(causal, fox_logits(kf, -f_new), NEG_BIG), vf, 'bhts,bshd->bhtd')

    def page_step(carry, phys):
        st_m, st_f, suffix = carry
        ckv_p = cache_ckv[phys]
        kpe_p = cache_kpe[phys]
        k_p = cache_fk[phys]
        v_p = cache_fv[phys]
        logf_p = cache_flogf[phys].astype(f32)
        after = lax.cumsum(logf_p, axis=1, reverse=True) - logf_p + suffix[:, None, :]
        st_m = online_step(st_m, mla_logits(ckv_p, kpe_p), ckv_p, 'bhts,bsr->bhtr')
        st_f = online_step(st_f, fox_logits(k_p, after.transpose(0, 2, 1)), v_p, 'bhts,bshd->bhtd')
        return (st_m, st_f, suffix + logf_p.sum(1)), None

    (st_mla, st_fox, _), _ = lax.scan(page_step, (st_mla, st_fox, jnp.zeros((Bd, H_FOX), f32)),
                                      page_table.T, reverse=True)
    o_lat = (st_mla[2] / st_mla[1][..., None]).astype(q_nope.dtype)
    o_mla = jnp.einsum('bhtr,rhd->bthd', o_lat, w_uv)
    o_fox = (st_fox[2] / st_fox[1][..., None]).transpose(0, 2, 1, 3).astype(qf.dtype)
    return jnp.concatenate([o_mla.reshape(Bd, T, MLA_WIDTH), o_fox.reshape(Bd, T, FOX_WIDTH)], axis=-1)


def swiglu(x, wg, wu, wd):
    return (jax.nn.silu(x @ wg) * (x @ wu)) @ wd


def routed_experts(h, idx, w, wg, wu, wd):
    T, D = h.shape
    A = T * TOP_K
    flat_e = idx.reshape(-1).astype(jnp.int32)
    order = jnp.argsort(flat_e)
    sorted_e = flat_e[order]
    counts = jnp.bincount(flat_e, length=N_EXPERTS)
    padded = (counts + EXPERT_BLOCK - 1) // EXPERT_BLOCK * EXPERT_BLOCK
    pad_end = jnp.cumsum(padded)
    pad_start = pad_end - padded
    start = jnp.cumsum(counts) - counts
    dest = pad_start[sorted_e] + (jnp.arange(A) - start[sorted_e])
    n_blocks = (A + EXPERT_BLOCK - 1) // EXPERT_BLOCK + N_EXPERTS
    n_rows = n_blocks * EXPERT_BLOCK
    row_tok = jnp.full((n_rows,), T, jnp.int32).at[dest].set((order // TOP_K).astype(jnp.int32))
    row_w = jnp.zeros((n_rows,), jnp.float32).at[dest].set(w.reshape(-1)[order].astype(jnp.float32))
    block_e = jnp.minimum(jnp.searchsorted(pad_end, jnp.arange(n_blocks) * EXPERT_BLOCK, side='right'), N_EXPERTS - 1)
    h_pad = jnp.concatenate([h, jnp.zeros((1, D), h.dtype)], axis=0)

    def block_step(y, xs):
        tok, rw, e = xs
        out = swiglu(h_pad[tok], wg[e], wu[e], wd[e]).astype(jnp.float32) * rw[:, None]
        return y.at[tok].add(out), None

    y, _ = lax.scan(block_step, jnp.zeros((T + 1, D), jnp.float32),
                    (row_tok.reshape(n_blocks, EXPERT_BLOCK), row_w.reshape(n_blocks, EXPERT_BLOCK), block_e))
    return y[:T].astype(h.dtype)


def moe_ffn(h, w_router, b_router, wg, wu, wd, sg, su, sd):
    T = h.shape[0]
    scores = jax.nn.sigmoid((h @ w_router).astype(jnp.float32))
    sel = scores + b_router.astype(jnp.float32)
    grp = sel.reshape(T, N_GROUPS, N_EXPERTS // N_GROUPS)
    grp_score = lax.top_k(grp, 2)[0].sum(-1)
    _, top_g = lax.top_k(grp_score, TOPK_GROUPS)
    gmask = jax.nn.one_hot(top_g, N_GROUPS, dtype=jnp.float32).sum(1) > 0
    emask = jnp.repeat(gmask, N_EXPERTS // N_GROUPS, axis=1)
    _, idx = lax.top_k(jnp.where(emask, sel, -jnp.inf), TOP_K)
    w = jnp.take_along_axis(scores, idx, axis=1)
    w = w / w.sum(-1, keepdims=True) * ROUTED_SCALE
    return swiglu(h, sg, su, sd) + routed_experts(h, idx, w, wg, wu, wd)


def moe_sublayer(x, shift, scale, gate, g_norm, w_router, b_router, wg, wu, wd, sg, su, sd):
    B, S, D = x.shape
    h = adaln(x, g_norm, shift, scale).reshape(B * S, D)
    y = moe_ffn(h, w_router, b_router, wg, wu, wd, sg, su, sd).reshape(B, S, D)
    return x + gate[:, None, :] * y


def setup_inputs(seed: int = 0) -> dict:
    key = jax.random.key(seed)
    ks = iter(jax.random.split(key, 40))
    f32 = jnp.float32
    n_pages = PAST_LEN // PAGE_SIZE
    n_pool = (DEC_BATCH * n_pages * 5) // 4

    def nrm(shape, scale=1.0):
        return jax.random.normal(next(ks), shape, f32) * scale

    def gain(shape):
        return 1.0 + nrm(shape, 0.02)

    page_table = jax.random.permutation(next(ks), n_pool)[:DEC_BATCH * n_pages].reshape(DEC_BATCH, n_pages).astype(jnp.int32)
    return {
        "x_prompt": nrm((BATCH, SEQ, D_MODEL)),
        "x_sample": nrm((DEC_BATCH, DEC_SEQ, D_MODEL)),
        "cache_mla_ckv": nrm((DEPTH, n_pool, PAGE_SIZE, MLA_KV_RANK)),
        "cache_mla_kpe": nrm((DEPTH, n_pool, PAGE_SIZE, MLA_ROPE_DIM)),
        "cache_fox_k": nrm((DEPTH, n_pool, PAGE_SIZE, H_FOX, FOX_HEAD_DIM)),
        "cache_fox_v": nrm((DEPTH, n_pool, PAGE_SIZE, H_FOX, FOX_HEAD_DIM)),
        "cache_fox_logf": jax.nn.log_sigmoid(FORGET_BIAS + nrm((DEPTH, n_pool, PAGE_SIZE, H_FOX))),
        "page_table": page_table,
        "c_prompt": nrm((BATCH, D_MODEL)),
        "c_sample": nrm((DEC_BATCH, D_MODEL)),
        "w_ada": nrm((DEPTH, D_MODEL, 6 * D_MODEL), 0.5 * D_MODEL ** -0.5),
        "b_ada": nrm((DEPTH, 6 * D_MODEL), 0.02),
        "g_norm_attn": gain((DEPTH, D_MODEL)),
        "g_norm_moe": gain((DEPTH, D_MODEL)),
        "w_in": nrm((DEPTH, D_MODEL, IN_WIDTH), D_MODEL ** -0.5),
        "g_q_a": gain((DEPTH, MLA_Q_RANK)),
        "w_q_b": nrm((DEPTH, MLA_Q_RANK, H_MLA * (MLA_NOPE_DIM + MLA_ROPE_DIM)), MLA_Q_RANK ** -0.5),
        "g_q_mla": gain((DEPTH, MLA_NOPE_DIM + MLA_ROPE_DIM)),
        "g_ckv": gain((DEPTH, MLA_KV_RANK)),
        "g_kpe": gain((DEPTH, MLA_ROPE_DIM)),
        "w_uk": nrm((DEPTH, MLA_KV_RANK, H_MLA, MLA_NOPE_DIM), MLA_KV_RANK ** -0.5),
        "w_uv": nrm((DEPTH, MLA_KV_RANK, H_MLA, MLA_V_DIM), MLA_KV_RANK ** -0.5),
        "g_q_fox": gain((DEPTH, FOX_HEAD_DIM)),
        "g_k_fox": gain((DEPTH, FOX_HEAD_DIM)),
        "b_forget": FORGET_BIAS + nrm((DEPTH, H_FOX), 0.3),
        "w_o": nrm((DEPTH, MLA_WIDTH + FOX_WIDTH, D_MODEL), (MLA_WIDTH + FOX_WIDTH) ** -0.5),
        "w_router": nrm((DEPTH, D_MODEL, N_EXPERTS), D_MODEL ** -0.5),
        "b_router": nrm((DEPTH, N_EXPERTS), 0.01),
        "w_exp_gate": nrm((DEPTH, N_EXPERTS, D_MODEL, EXPERT_FF), D_MODEL ** -0.5),
        "w_exp_up": nrm((DEPTH, N_EXPERTS, D_MODEL, EXPERT_FF), D_MODEL ** -0.5),
        "w_exp_down": nrm((DEPTH, N_EXPERTS, EXPERT_FF, D_MODEL), EXPERT_FF ** -0.5),
        "w_sh_gate": nrm((DEPTH, D_MODEL, SHARED_FF), D_MODEL ** -0.5),
        "w_sh_up": nrm((DEPTH, D_MODEL, SHARED_FF), D_MODEL ** -0.5),
        "w_sh_down": nrm((DEPTH, SHARED_FF, D_MODEL), SHARED_FF ** -0.5),
    }


def reference(x_prompt, x_sample, cache_mla_ckv, cache_mla_kpe, cache_fox_k, cache_fox_v, cache_fox_logf,
              page_table, c_prompt, c_sample, w_ada, b_ada, g_norm_attn, g_norm_moe, w_in, g_q_a, w_q_b,
              g_q_mla, g_ckv, g_kpe, w_uk, w_uv, g_q_fox, g_k_fox, b_forget, w_o, w_router, b_router,
              w_exp_gate, w_exp_up, w_exp_down, w_sh_gate, w_sh_up, w_sh_down):
    seq = x_prompt.shape[1]
    dec_seq = x_sample.shape[1]
    past_len = page_table.shape[1] * cache_mla_ckv.shape[2]
    pos_p = jnp.arange(seq)
    pos_s = past_len + jnp.arange(dec_seq)
    xp, xs = x_prompt, x_sample
    st_p = [[], [], [], [], []]
    st_s = [[], [], [], [], []]
    for l in range(DEPTH):
        proj_w = (w_in[l], g_q_a[l], w_q_b[l], g_q_mla[l], g_ckv[l], g_kpe[l], g_q_fox[l], g_k_fox[l], b_forget[l])
        moe_w = (w_router[l], b_router[l], w_exp_gate[l], w_exp_up[l], w_exp_down[l], w_sh_gate[l], w_sh_up[l], w_sh_down[l])
        sh1, sc1, gt1, sh2, sc2, gt2 = modulation(c_prompt, w_ada[l], b_ada[l])
        q_nope, q_pe, ckv, kpe, qf, kf, vf, logf = project(adaln(xp, g_norm_attn[l], sh1, sc1), pos_p, *proj_w)
        mix = prompt_mixers(q_nope, q_pe, ckv, kpe, qf, kf, vf, logf, w_uk[l], w_uv[l])
        xp = xp + gt1[:, None, :] * (mix @ w_o[l])
        xp = moe_sublayer(xp, sh2, sc2, gt2, g_norm_moe[l], *moe_w)
        for lst, arr in zip(st_p, (ckv, kpe, kf, vf, logf)):
            lst.append(arr)
        sh1, sc1, gt1, sh2, sc2, gt2 = modulation(c_sample, w_ada[l], b_ada[l])
        q_nope, q_pe, ckv, kpe, qf, kf, vf, logf = project(adaln(xs, g_norm_attn[l], sh1, sc1), pos_s, *proj_w)
        mix = sample_mixers(q_nope, q_pe, ckv, kpe, qf, kf, vf, logf,
                            cache_mla_ckv[l], cache_mla_kpe[l], cache_fox_k[l], cache_fox_v[l], cache_fox_logf[l],
                            page_table, w_uk[l], w_uv[l])
        xs = xs + gt1[:, None, :] * (mix @ w_o[l])
        xs = moe_sublayer(xs, sh2, sc2, gt2, g_norm_moe[l], *moe_w)
        for lst, arr in zip(st_s, (ckv, kpe, kf, vf, logf)):
            lst.append(arr)
    new_ckv_p, new_kpe_p, new_fk_p, new_fv_p, new_flogf_p = [jnp.stack(a, 0) for a in st_p]
    new_ckv_s, new_kpe_s, new_fk_s, new_fv_s, new_flogf_s = [jnp.stack(a, 0) for a in st_s]
    return (xp, xs, new_ckv_p, new_kpe_p, new_fk_p, new_fv_p, new_flogf_p,
            new_ckv_s, new_kpe_s, new_fk_s, new_fv_s, new_flogf_s)
```

```python
import functools

import jax
import jax.numpy as jnp
from jax import lax
from jax.experimental import pallas as pl
from jax.experimental.pallas import tpu as pltpu

f32 = jnp.float32
bf16 = jnp.bfloat16
i32 = jnp.int32
u32 = jnp.uint32

LANES = 128
NOPE = 128
ROPE = 64
HEAD = 128
MLA_QK = 2 * LANES
MLA_SCALE = (NOPE + ROPE) ** -0.5
FOX_SCALE = HEAD ** -0.5
ROPE_THETA = 10000.0
N_GROUPS = 8
TOPK_GROUPS = 4
TOP_K = 8
ROUTED_SCALE = 2.5
NORM_EPS = 1e-6
NEG_INF = float("-inf")

ROW_TILE = 256
ATT_TILE = 256
EXPERT_BLOCK = 256
COMBINE_TILE = 128
DECODE_PAGES = 8
MOD_TILE = 1024


def _params(n_axes, vmem_mb):
    return pltpu.CompilerParams(dimension_semantics=("arbitrary",) * n_axes,
                                vmem_limit_bytes=vmem_mb << 20)


def _const_spec(shape):
    nd = len(shape)
    return pl.BlockSpec(shape, lambda *_: (0,) * nd, pipeline_mode=pl.Buffered(1))


def _mod_spec(mod_rows, tm, d, rows_per_mod):
    if mod_rows == 1:
        tiles_per_mod = rows_per_mod // tm
        return pl.BlockSpec((1, 1, d), lambda i: (i // tiles_per_mod, 0, 0))
    return pl.BlockSpec((1, tm, d), lambda i: (0, i, 0))


def _sigmoid(x):
    return 1.0 / (1.0 + jnp.exp(-x))


def _silu(x):
    return x * _sigmoid(x)


def _rms(x, width):
    return lax.rsqrt(jnp.sum(x * x, axis=-1, keepdims=True) * (1.0 / width) + NORM_EPS)


def _dot(a, b):
    return jnp.dot(a, b, preferred_element_type=f32)


def _dot_nt(a, b):
    return lax.dot_general(a, b, (((1,), (1,)), ((), ())), preferred_element_type=f32)


def _split3(x):
    hi = x.astype(bf16)
    r = x - hi.astype(f32)
    mid = r.astype(bf16)
    lo = (r - mid.astype(f32)).astype(bf16)
    return hi, mid, lo


def _mod_kernel(c_ref, w_ref, b_ref, o_ref):
    a = _silu(c_ref[...]).astype(bf16)
    o_ref[...] = _dot(a, w_ref[...].astype(bf16)) + b_ref[...]


def _modulation(c, w_ada, b_ada):
    m, d = c.shape
    n = w_ada.shape[1]
    tn = min(MOD_TILE, n)
    return pl.pallas_call(
        _mod_kernel,
        grid=(n // tn,),
        in_specs=[pl.BlockSpec((m, d), lambda j: (0, 0)),
                  pl.BlockSpec((d, tn), lambda j: (0, j)),
                  pl.BlockSpec((1, tn), lambda j: (0, j))],
        out_specs=pl.BlockSpec((m, tn), lambda j: (0, j)),
        out_shape=jax.ShapeDtypeStruct((m, n), f32),
        compiler_params=_params(1, 40),
        name="modulation",
    )(c, w_ada, b_ada.reshape(1, n))


def _proj_kernel(x_ref, sh_ref, sc_ref, gn_ref, win_ref, gqa_ref, wqb_ref, gckv_ref, wuk_ref, wuv_ref,
                 small_ref, cos_ref, sin_ref,
                 ckv_ref, kpe_ref, kf_ref, vf_ref, logf_ref,
                 qm_ref, km_ref, vm_ref, qfb_ref, kfb_ref, vfb_ref, *, dims):
    d, qr, kr, n_mla, n_fox = dims
    fw = n_fox * HEAD
    x = x_ref[...]
    h = x * _rms(x, d) * gn_ref[...]
    h = h * (1.0 + sc_ref[0]) + sh_ref[0]
    hb = h.astype(bf16)
    cos = cos_ref[...]
    sin = sin_ref[...]
    g_qn, g_pea, g_peb = small_ref[0:1, :], small_ref[1:2, :], small_ref[2:3, :]
    g_ka, g_kb = small_ref[3:4, :], small_ref[4:5, :]
    g_qf, g_kf, b_f = small_ref[5:6, :], small_ref[6:7, :], small_ref[7:8, :]

    o_ckv = qr
    o_qf = o_ckv + kr
    o_kf = o_qf + fw
    o_vf = o_kf + fw
    o_ka = o_vf + fw
    o_kb = o_ka + LANES
    o_f = o_kb + LANES

    ka = _dot(hb, win_ref[:, o_ka:o_ka + LANES])
    kb = _dot(hb, win_ref[:, o_kb:o_kb + LANES])
    r = _rms(ka, ROPE)
    kpe = (ka * r * g_ka) * cos + (kb * r * g_kb) * sin
    kpe_ref[...] = kpe[:, :ROPE]
    kpe_b = kpe.astype(bf16)

    c = _dot(hb, win_ref[:, o_ckv:o_ckv + kr])
    cn = c * _rms(c, kr) * gckv_ref[...]
    ckv_ref[...] = cn
    cb = cn.astype(bf16)
    k_nope = _dot(cb, wuk_ref[...])
    vm_ref[...] = _dot(cb, wuv_ref[...]).astype(bf16)

    qa = _dot(hb, win_ref[:, 0:qr])
    qa = (qa * _rms(qa, qr) * gqa_ref[...]).astype(bf16)
    hw = n_mla * LANES
    for hd in range(n_mla):
        lo = hd * LANES
        qn = _dot(qa, wqb_ref[:, lo:lo + LANES])
        pa = _dot(qa, wqb_ref[:, hw + lo:hw + lo + LANES])
        pb = _dot(qa, wqb_ref[:, 2 * hw + lo:2 * hw + lo + LANES])
        ss = jnp.sum(qn * qn, axis=-1, keepdims=True) + jnp.sum(pa * pa, axis=-1, keepdims=True)
        r = lax.rsqrt(ss * (1.0 / (NOPE + ROPE)) + NORM_EPS)
        q_nope = qn * r * g_qn
        q_pe = (pa * r * g_pea) * cos + (pb * r * g_peb) * sin
        base = hd * MLA_QK
        qm_ref[:, base:base + LANES] = (q_nope * MLA_SCALE).astype(bf16)
        qm_ref[:, base + LANES:base + MLA_QK] = (q_pe * MLA_SCALE).astype(bf16)
        km_ref[:, base:base + LANES] = k_nope[:, lo:lo + LANES].astype(bf16)
        km_ref[:, base + LANES:base + MLA_QK] = kpe_b

    for hd in range(n_fox):
        lo = hd * HEAD
        qf = _dot(hb, win_ref[:, o_qf + lo:o_qf + lo + HEAD])
        qfb_ref[:, lo:lo + HEAD] = (qf * _rms(qf, HEAD) * g_qf * FOX_SCALE).astype(bf16)
        kf = _dot(hb, win_ref[:, o_kf + lo:o_kf + lo + HEAD])
        kf = kf * _rms(kf, HEAD) * g_kf
        kf_ref[:, lo:lo + HEAD] = kf
        kfb_ref[:, lo:lo + HEAD] = kf.astype(bf16)
    vf = _dot(hb, win_ref[:, o_vf:o_vf + fw])
    vf_ref[...] = vf
    vfb_ref[...] = vf.astype(bf16)
    z = _dot(hb, win_ref[:, o_f:o_f + LANES]) + b_f
    logf = jnp.minimum(z, 0.0) - jnp.log1p(jnp.exp(-jnp.abs(z)))
    logf_ref[...] = logf[:, :n_fox]


def _project(x, shift, scale, g_norm, pw, cos_t, sin_t, rows_per_mod, rows_per_pos):
    t, d = x.shape
    qr, kr, n_mla, n_fox = pw["dims"]
    fw = n_fox * HEAD
    tm = min(ROW_TILE, t)
    tiles_per_pos = max(rows_per_pos // tm, 1)
    pos_rows = cos_t.shape[0]
    tp = min(tm, pos_rows)
    mod_spec = _mod_spec(shift.shape[1], tm, d, rows_per_mod)
    pos_spec = pl.BlockSpec((tp, LANES), lambda i: (i % tiles_per_pos, 0))

    def row_spec(w):
        return pl.BlockSpec((tm, w), lambda i: (i, 0))

    out_widths = [(kr, f32), (ROPE, f32), (fw, f32), (fw, f32), (n_fox, f32),
                  (n_mla * MLA_QK, bf16), (n_mla * MLA_QK, bf16), (n_mla * HEAD, bf16),
                  (fw, bf16), (fw, bf16), (fw, bf16)]
    return pl.pallas_call(
        functools.partial(_proj_kernel, dims=(d, qr, kr, n_mla, n_fox)),
        grid=(t // tm,),
        in_specs=[row_spec(d), mod_spec, mod_spec, _const_spec((1, d)),
                  _const_spec(pw["w_in"].shape), _const_spec((1, qr)), _const_spec(pw["w_qb"].shape),
                  _const_spec((1, kr)), _const_spec(pw["w_uk"].shape), _const_spec(pw["w_uv"].shape),
                  _const_spec((8, LANES)), pos_spec, pos_spec],
        out_specs=[row_spec(w) for w, _ in out_widths],
        out_shape=[jax.ShapeDtypeStruct((t, w), dt) for w, dt in out_widths],
        compiler_params=_params(1, 56),
        name="project",
    )(x, shift, scale, g_norm, pw["w_in"], pw["g_qa"], pw["w_qb"], pw["g_ckv"], pw["w_uk"], pw["w_uv"],
      pw["small"], cos_t, sin_t)


def _pad_lanes(a):
    return jnp.pad(a, [(0, 0)] * (a.ndim - 1) + [(0, LANES - a.shape[-1])])


def _rot_half(a):
    half = a.shape[-1] // 2
    return jnp.concatenate([a[..., half:], a[..., :half]], axis=-1)


def _pack_proj_weights(w_in, g_q_a, w_q_b, g_q_mla, g_ckv, g_kpe, w_uk, w_uv, g_q_fox, g_k_fox, b_forget):
    qr = g_q_a.shape[0]
    kr = g_ckv.shape[0]
    n_mla = w_uk.shape[1]
    n_fox = b_forget.shape[0]
    fw = n_fox * HEAD
    o = 0
    w_qa = w_in[:, o:o + qr]; o += qr
    w_ckv = w_in[:, o:o + kr]; o += kr
    w_kpe = w_in[:, o:o + ROPE]; o += ROPE
    w_fox = w_in[:, o:o + 3 * fw]; o += 3 * fw
    w_f = w_in[:, o:o + n_fox]
    w_in_p = jnp.concatenate([w_qa, w_ckv, w_fox, _pad_lanes(w_kpe), _pad_lanes(_rot_half(w_kpe)),
                              _pad_lanes(w_f)], axis=1).astype(bf16)
    wq = w_q_b.reshape(qr, n_mla, NOPE + ROPE)
    w_nope = wq[:, :, :NOPE].reshape(qr, n_mla * NOPE)
    w_pe = wq[:, :, NOPE:]
    w_pea = _pad_lanes(w_pe).reshape(qr, n_mla * LANES)
    w_peb = _pad_lanes(_rot_half(w_pe)).reshape(qr, n_mla * LANES)
    w_qb_p = jnp.concatenate([w_nope, w_pea, w_peb], axis=1).astype(bf16)
    g_pe = g_q_mla[NOPE:]
    small = jnp.stack([g_q_mla[:NOPE], _pad_lanes(g_pe), _pad_lanes(_rot_half(g_pe)),
                       _pad_lanes(g_kpe), _pad_lanes(_rot_half(g_kpe)),
                       g_q_fox, g_k_fox, _pad_lanes(b_forget)]).astype(f32)
    return {"dims": (qr, kr, n_mla, n_fox), "w_in": w_in_p, "g_qa": g_q_a.reshape(1, qr), "w_qb": w_qb_p,
            "g_ckv": g_ckv.reshape(1, kr), "w_uk": w_uk.reshape(kr, n_mla * NOPE).astype(bf16),
            "w_uv": w_uv.reshape(kr, n_mla * HEAD).astype(bf16), "small": small}


def _rope_tables(pos):
    half = ROPE // 2
    inv = ROPE_THETA ** (-jnp.arange(half, dtype=f32) / half)
    ang = pos.astype(f32)[:, None] * inv[None, :]
    cos, sin = jnp.cos(ang), jnp.sin(ang)
    return _pad_lanes(jnp.concatenate([cos, cos], axis=1)), _pad_lanes(jnp.concatenate([-sin, sin], axis=1))


def _cumsum_kernel(x_ref, o_ref):
    s = x_ref.shape[1]
    x = x_ref[0]
    tri = (lax.broadcasted_iota(i32, (s, s), 1) <= lax.broadcasted_iota(i32, (s, s), 0)).astype(bf16)
    hi, mid, lo = _split3(x)
    o_ref[0] = _dot(tri, hi) + _dot(tri, mid) + _dot(tri, lo)


def _cumsum_seq(x):
    b, s, h = x.shape
    return pl.pallas_call(
        _cumsum_kernel,
        grid=(b,),
        in_specs=[pl.BlockSpec((1, s, h), lambda i: (i, 0, 0))],
        out_specs=pl.BlockSpec((1, s, h), lambda i: (i, 0, 0)),
        out_shape=jax.ShapeDtypeStruct((b, s, h), f32),
        compiler_params=_params(1, 40),
        name="cumsum_logf",
    )(x)


def _flash_kernel(*refs, has_bias):
    if has_bias:
        q_ref, k_ref, v_ref, cq_ref, ck_ref, o_ref = refs
    else:
        q_ref, k_ref, v_ref, o_ref = refs
    tq = q_ref.shape[0]
    dv = v_ref.shape[1]
    i = pl.program_id(2)
    q = q_ref[...]

    def step(j, carry, diagonal):
        m, l, acc = carry
        start = pl.multiple_of(j * tq, tq)
        s = _dot_nt(q, k_ref[pl.ds(start, tq), :])
        if has_bias:
            s = s + (cq_ref[0, 0] - ck_ref[0, 0, pl.ds(j, 1), :])
        if diagonal:
            keep = lax.broadcasted_iota(i32, (tq, tq), 1) <= lax.broadcasted_iota(i32, (tq, tq), 0)
            s = jnp.where(keep, s, NEG_INF)
        m_new = jnp.maximum(m, jnp.max(s, axis=-1, keepdims=True))
        p = jnp.exp(s - m_new)
        corr = jnp.exp(m - m_new)
        l = l * corr + jnp.sum(p, axis=-1, keepdims=True)
        acc = acc * corr + _dot(p.astype(bf16), v_ref[pl.ds(start, tq), :])
        return m_new, l, acc

    init = (jnp.full((tq, 1), NEG_INF, f32), jnp.zeros((tq, 1), f32), jnp.zeros((tq, dv), f32))
    carry = step(i, init, True)
    m, l, acc = lax.fori_loop(0, i, lambda j, c: step(j, c, False), carry)
    o_ref[...] = (acc / l).astype(o_ref.dtype)


def _flash(q, k, v, batch, seq, n_heads, dk, dv, cum=None):
    t = q.shape[0]
    tq = min(ATT_TILE, seq)
    nq = seq // tq
    in_specs = [pl.BlockSpec((tq, dk), lambda b, h, i: (b * nq + i, h)),
                pl.BlockSpec((seq, dk), lambda b, h, i: (b, h)),
                pl.BlockSpec((seq, dv), lambda b, h, i: (b, h))]
    args = [q, k, v]
    if cum is not None:
        cum_t = cum.transpose(0, 2, 1)
        in_specs += [pl.BlockSpec((1, 1, tq, 1), lambda b, h, i: (b, h, i, 0)),
                     pl.BlockSpec((1, 1, nq, tq), lambda b, h, i: (b, h, 0, 0))]
        args += [cum_t.reshape(batch, n_heads, seq, 1), cum_t.reshape(batch, n_heads, nq, tq)]
    return pl.pallas_call(
        functools.partial(_flash_kernel, has_bias=cum is not None),
        grid=(batch, n_heads, nq),
        in_specs=in_specs,
        out_specs=pl.BlockSpec((tq, dv), lambda b, h, i: (b * nq + i, h)),
        out_shape=jax.ShapeDtypeStruct((t, n_heads * dv), bf16),
        compiler_params=_params(3, 40),
        name="flash_bias" if cum is not None else "flash",
    )(*args)


def _absorb_kernel(q_ref, wuk_ref, o_ref, *, n_mla):
    kr = wuk_ref.shape[0]
    for hd in range(n_mla):
        qn = q_ref[:, hd * MLA_QK:hd * MLA_QK + NOPE]
        o_ref[:, hd * kr:(hd + 1) * kr] = _dot_nt(qn, wuk_ref[:, hd * NOPE:(hd + 1) * NOPE]).astype(bf16)


def _absorb(qm, w_uk2, n_mla):
    bd = qm.shape[0]
    kr = w_uk2.shape[0]
    return pl.pallas_call(
        functools.partial(_absorb_kernel, n_mla=n_mla),
        out_shape=jax.ShapeDtypeStruct((bd, n_mla * kr), bf16),
        name="absorb_query",
    )(qm, w_uk2)


def _expand_kernel(olat_ref, ofox_ref, wuv_ref, mla_ref, fox_ref, *, n_mla):
    kr = wuv_ref.shape[0]
    for hd in range(n_mla):
        o = olat_ref[:, hd * kr:(hd + 1) * kr].astype(bf16)
        mla_ref[:, hd * HEAD:(hd + 1) * HEAD] = _dot(o, wuv_ref[:, hd * HEAD:(hd + 1) * HEAD]).astype(bf16)
    fox_ref[...] = ofox_ref[...].astype(bf16)


def _expand(o_lat, o_fox, w_uv2, n_mla):
    bd = o_lat.shape[0]
    return pl.pallas_call(
        functools.partial(_expand_kernel, n_mla=n_mla),
        out_shape=[jax.ShapeDtypeStruct((bd, n_mla * HEAD), bf16),
                   jax.ShapeDtypeStruct(o_fox.shape, bf16)],
        name="expand_latent",
    )(o_lat, o_fox, w_uv2)


def _decode_kernel(pt_ref, qlat_ref, qpe_ref, qf_ref, fnew_ref, ckvn_ref, kpen_ref, kfn_ref, vfn_ref, *rest,
                   n_pages_step, n_fox):
    g = n_pages_step
    ckv_refs, kpe_refs = rest[0:g], rest[g:2 * g]
    fk_refs, fv_refs, lf_refs = rest[2 * g:3 * g], rest[3 * g:4 * g], rest[4 * g:5 * g]
    olat_ref, ofox_ref = rest[5 * g], rest[5 * g + 1]
    m_m, l_m, acc_m, m_f, l_f, acc_f, suf = rest[5 * g + 2:]
    j = pl.program_id(1)
    page = ckv_refs[0].shape[1]
    fw = qf_ref.shape[2]

    q_lat = qlat_ref[0]
    q_pe = qpe_ref[0][:, :ROPE]
    head_of_col = lax.broadcasted_iota(i32, (n_fox, fw), 1) // HEAD
    own = head_of_col == lax.broadcasted_iota(i32, (n_fox, fw), 0)
    q_fox_f = jnp.where(own, jnp.broadcast_to(qf_ref[0].astype(f32), (n_fox, fw)), 0.0)
    q_fox = q_fox_f.astype(bf16)
    f_new = fnew_ref[0]

    @pl.when(j == 0)
    def _():
        ckv_n = ckvn_ref[0].astype(bf16).astype(f32)
        kpe_n = kpen_ref[0].astype(bf16).astype(f32)
        s_m = (jnp.sum(q_lat.astype(f32) * ckv_n, axis=-1, keepdims=True)
               + jnp.sum(q_pe.astype(f32) * kpe_n, axis=-1, keepdims=True))
        m_m[...] = s_m
        l_m[...] = jnp.ones_like(s_m)
        acc_m[...] = jnp.broadcast_to(ckv_n, acc_m.shape)
        kf_n = kfn_ref[0].astype(bf16).astype(f32)
        s_f = jnp.sum(q_fox_f * kf_n, axis=-1, keepdims=True)
        m_f[...] = s_f
        l_f[...] = jnp.ones_like(s_f)
        acc_f[...] = jnp.broadcast_to(vfn_ref[0].astype(bf16).astype(f32), acc_f.shape)
        suf[...] = jnp.zeros_like(suf)

    key_row = lax.broadcasted_iota(i32, (page, 2 * page), 0)
    key_col = lax.broadcasted_iota(i32, (page, 2 * page), 1)
    later = jnp.logical_or(key_row > key_col, key_col >= page).astype(bf16)
    suffix = suf[...]
    s_mla, s_fox, ckv_b, fv_b = [], [], [], []
    for p in range(g):
        cb = ckv_refs[p][0].astype(bf16)
        ckv_b.append(cb)
        s_mla.append(_dot_nt(q_lat, cb) + _dot_nt(q_pe, kpe_refs[p][0].astype(bf16)))
        lf = lf_refs[p][0]
        hi, mid, lo = _split3(lf)
        tn = (((0,), (0,)), ((), ()))
        sums = (lax.dot_general(hi, later, tn, preferred_element_type=f32)
                + lax.dot_general(mid, later, tn, preferred_element_type=f32)
                + lax.dot_general(lo, later, tn, preferred_element_type=f32))
        after = sums[:, :page]
        s_fox.append(_dot_nt(q_fox, fk_refs[p][0].astype(bf16)) + f_new + (after + suffix))
        suffix = suffix + sums[:, page:page + 1]
        fv_b.append(fv_refs[p][0].astype(bf16))
    suf[...] = suffix

    def update(s_list, vals, m_ref, l_ref, acc_ref):
        s = jnp.concatenate(s_list, axis=1)
        m_old = m_ref[...]
        m_new = jnp.maximum(m_old, jnp.max(s, axis=-1, keepdims=True))
        corr = jnp.exp(m_old - m_new)
        pr = jnp.exp(s - m_new)
        l_ref[...] = l_ref[...] * corr + jnp.sum(pr, axis=-1, keepdims=True)
        pb = pr.astype(bf16)
        acc = acc_ref[...] * corr
        for p in range(g):
            acc = acc + _dot(pb[:, p * page:(p + 1) * page], vals[p])
        acc_ref[...] = acc
        m_ref[...] = m_new

    update(s_mla, ckv_b, m_m, l_m, acc_m)
    update(s_fox, fv_b, m_f, l_f, acc_f)

    @pl.when(j == pl.num_programs(1) - 1)
    def _():
        olat_ref[0] = acc_m[...] / l_m[...]
        o = jnp.where(own, acc_f[...] / l_f[...], 0.0)
        ofox_ref[0] = jnp.sum(o, axis=0, keepdims=True)


def _decode(page_table, q_lat, q_pe, q_fox, f_new, ckv_new, kpe_new, kf_new, vf_new,
            cache_ckv, cache_kpe, cache_fk, cache_fv, cache_logf):
    bd, n_pages = page_table.shape
    n_mla, kr = q_lat.shape[1:]
    fw = q_fox.shape[2]
    n_fox = fw // HEAD
    page = cache_ckv.shape[1]
    g = min(DECODE_PAGES, n_pages)
    steps = n_pages // g

    def per_batch(shape):
        nd = len(shape)
        return pl.BlockSpec((1,) + tuple(shape[1:]), lambda b, j, pt: (b,) + (0,) * (nd - 1))

    def paged(width, p):
        return pl.BlockSpec((1, page, width), lambda b, j, pt: (pt[b, n_pages - 1 - (j * g + p)], 0, 0))

    small_in = [q_lat, q_pe, q_fox, f_new, ckv_new, kpe_new, kf_new, vf_new]
    in_specs = [per_batch(a.shape) for a in small_in]
    caches = []
    for arr, width in ((cache_ckv, kr), (cache_kpe, ROPE), (cache_fk, fw), (cache_fv, fw), (cache_logf, n_fox)):
        in_specs += [paged(width, p) for p in range(g)]
        caches += [arr] * g
    grid_spec = pltpu.PrefetchScalarGridSpec(
        num_scalar_prefetch=1,
        grid=(bd, steps),
        in_specs=in_specs,
        out_specs=[pl.BlockSpec((1, n_mla, kr), lambda b, j, pt: (b, 0, 0)),
                   pl.BlockSpec((1, 1, fw), lambda b, j, pt: (b, 0, 0))],
        scratch_shapes=[pltpu.VMEM((n_mla, 1), f32), pltpu.VMEM((n_mla, 1), f32), pltpu.VMEM((n_mla, kr), f32),
                        pltpu.VMEM((n_fox, 1), f32), pltpu.VMEM((n_fox, 1), f32), pltpu.VMEM((n_fox, fw), f32),
                        pltpu.VMEM((n_fox, 1), f32)],
    )
    return pl.pallas_call(
        functools.partial(_decode_kernel, n_pages_step=g, n_fox=n_fox),
        grid_spec=grid_spec,
        out_shape=[jax.ShapeDtypeStruct((bd, n_mla, kr), f32), jax.ShapeDtypeStruct((bd, 1, fw), f32)],
        compiler_params=_params(2, 48),
        name="paged_decode",
    )(page_table, *small_in, *caches)


def _oproj_kernel(mla_ref, fox_ref, woa_ref, wob_ref, x_ref, gt_ref, sh_ref, sc_ref, gn_ref, wr_ref, br_ref,
                  x1_ref, hp_ref, idx_ref, wt_ref, rank_ref, cnt_ref, carry_ref):
    d = x_ref.shape[1]
    tm = x_ref.shape[0]
    n_exp = wr_ref.shape[0]
    per_group = n_exp // N_GROUPS

    @pl.when(pl.program_id(0) == 0)
    def _():
        carry_ref[...] = jnp.zeros_like(carry_ref)

    mix = _dot(mla_ref[...], woa_ref[...]) + _dot(fox_ref[...], wob_ref[...])
    x1 = x_ref[...] + gt_ref[0] * mix
    x1_ref[...] = x1
    h = x1 * _rms(x1, d) * gn_ref[...]
    h = h * (1.0 + sc_ref[0]) + sh_ref[0]
    hp_ref[...] = h

    h_hi = h.astype(bf16)
    h_lo = (h - h_hi.astype(f32)).astype(bf16)
    wr = wr_ref[...]
    w_hi = wr.astype(bf16)
    w_lo = (wr - w_hi.astype(f32)).astype(bf16)
    logits = _dot_nt(w_hi, h_hi) + (_dot_nt(w_hi, h_lo) + _dot_nt(w_lo, h_hi))
    scores = _sigmoid(logits).reshape(N_GROUPS, per_group, tm)
    sel = scores + br_ref[...].reshape(N_GROUPS, per_group, 1)

    in_group = lax.broadcasted_iota(i32, (N_GROUPS, per_group, tm), 1).astype(f32)
    group_id = lax.broadcasted_iota(i32, (N_GROUPS, per_group, tm), 0).astype(f32)
    expert_id = group_id * per_group + in_group

    m1 = jnp.max(sel, axis=1, keepdims=True)
    first = jnp.min(jnp.where(sel == m1, in_group, float(per_group)), axis=1, keepdims=True)
    m2 = jnp.max(jnp.where(in_group == first, NEG_INF, sel), axis=1, keepdims=True)
    gscore = m1 + m2
    gid = lax.broadcasted_iota(i32, (N_GROUPS, 1, tm), 0).astype(f32)
    gkeep = jnp.zeros((N_GROUPS, 1, tm), f32)
    for _ in range(TOPK_GROUPS):
        best = jnp.max(gscore, axis=0, keepdims=True)
        pick = jnp.min(jnp.where(gscore == best, gid, float(N_GROUPS)), axis=0, keepdims=True)
        hit = gid == pick
        gkeep = jnp.where(hit, 1.0, gkeep)
        gscore = jnp.where(hit, NEG_INF, gscore)

    cand = jnp.where(jnp.broadcast_to(gkeep, sel.shape) > 0.5, sel, NEG_INF)
    chosen = jnp.zeros((N_GROUPS, per_group, tm), f32)
    picks, weights = [], []
    for _ in range(TOP_K):
        best = jnp.max(jnp.max(cand, axis=1, keepdims=True), axis=0, keepdims=True)
        pick = jnp.min(jnp.min(jnp.where(cand == best, expert_id, float(n_exp)), axis=1, keepdims=True),
                       axis=0, keepdims=True)
        hit = expert_id == pick
        weights.append(jnp.sum(jnp.sum(jnp.where(hit, scores, 0.0), axis=1, keepdims=True), axis=0))
        picks.append(pick)
        chosen = jnp.where(hit, 1.0, chosen)
        cand = jnp.where(hit, NEG_INF, cand)

    wsum = weights[0]
    for w in weights[1:]:
        wsum = wsum + w
    chosen2 = chosen.reshape(n_exp, tm)
    upto = (lax.broadcasted_iota(i32, (tm, tm), 0) <= lax.broadcasted_iota(i32, (tm, tm), 1)).astype(bf16)
    incl = _dot(chosen2.astype(bf16), upto)
    before = (carry_ref[...] + (incl - chosen2)).reshape(N_GROUPS, per_group, tm)
    carry_ref[...] = carry_ref[...] + jnp.sum(chosen2, axis=1, keepdims=True)
    cnt_ref[...] = carry_ref[...]
    for k in range(TOP_K):
        hit = expert_id == picks[k]
        rk = jnp.sum(jnp.sum(jnp.where(hit, before, 0.0), axis=1, keepdims=True), axis=0)
        idx_ref[k:k + 1, :] = picks[k][0].astype(i32)
        wt_ref[k:k + 1, :] = weights[k] / wsum * ROUTED_SCALE
        rank_ref[k:k + 1, :] = rk.astype(i32)


def _oproj_router(mla, fox, w_o_a, w_o_b, x, gate, shift, scale, g_norm, w_router_t, b_router, rows_per_mod):
    t, d = x.shape
    tm = min(ROW_TILE, t)
    n_exp = w_router_t.shape[0]
    mod_spec = _mod_spec(shift.shape[1], tm, d, rows_per_mod)

    def row_spec(w):
        return pl.BlockSpec((tm, w), lambda i: (i, 0))

    def tok_spec():
        return pl.BlockSpec((TOP_K, tm), lambda i: (0, i))

    return pl.pallas_call(
        _oproj_kernel,
        grid=(t // tm,),
        in_specs=[row_spec(mla.shape[1]), row_spec(fox.shape[1]), _const_spec(w_o_a.shape), _const_spec(w_o_b.shape),
                  row_spec(d), mod_spec, mod_spec, mod_spec, _const_spec((1, d)),
                  _const_spec((n_exp, d)), _const_spec((n_exp, 1))],
        out_specs=[row_spec(d), row_spec(d), tok_spec(), tok_spec(), tok_spec(),
                   pl.BlockSpec((n_exp, 1), lambda i: (0, 0))],
        out_shape=[jax.ShapeDtypeStruct((t, d), f32), jax.ShapeDtypeStruct((t, d), f32),
                   jax.ShapeDtypeStruct((TOP_K, t), i32), jax.ShapeDtypeStruct((TOP_K, t), f32),
                   jax.ShapeDtypeStruct((TOP_K, t), i32), jax.ShapeDtypeStruct((n_exp, 1), f32)],
        scratch_shapes=[pltpu.VMEM((n_exp, 1), f32)],
        compiler_params=_params(1, 48),
        name="oproj_router",
    )(mla, fox, w_o_a, w_o_b, x, gate, shift, scale, g_norm, w_router_t, b_router)


def _dispatch_kernel(dest_ref, hp_ref, xs_in_ref, xs_ref, sem):
    del xs_in_ref
    tm = hp_ref.shape[0]

    def row_copy(r, k):
        return pltpu.make_async_copy(hp_ref.at[pl.ds(r, 1)], xs_ref.at[pl.ds(dest_ref[k, r], 1)], sem)

    def issue(r, c):
        for k in range(TOP_K):
            row_copy(r, k).start()
        return c

    lax.fori_loop(0, tm, issue, 0)

    def drain(r, c):
        for k in range(TOP_K):
            row_copy(r, k).wait()
        return c

    lax.fori_loop(0, tm, drain, 0)


def _dispatch(dest, hp, n_rows):
    t, d = hp.shape
    tm = min(ROW_TILE, t)
    xs0 = jnp.zeros((n_rows, d), f32)
    return pl.pallas_call(
        _dispatch_kernel,
        grid=(t // tm,),
        in_specs=[pl.BlockSpec((TOP_K, tm), lambda i: (0, i), memory_space=pltpu.SMEM),
                  pl.BlockSpec((tm, d), lambda i: (i, 0)),
                  pl.BlockSpec(memory_space=pl.ANY)],
        out_specs=pl.BlockSpec(memory_space=pl.ANY),
        out_shape=jax.ShapeDtypeStruct((n_rows, d), f32),
        scratch_shapes=[pltpu.SemaphoreType.DMA(())],
        input_output_aliases={2: 0},
        compiler_params=_params(1, 32),
        name="dispatch_rows",
    )(dest, hp, xs0)


def _expert_kernel(be_ref, nu_ref, xs_ref, wg_ref, wu_ref, wd_ref, o_ref, wgb, wub, wdb):
    i = pl.program_id(0)
    prev = be_ref[jnp.maximum(i - 1, 0)]
    fresh = jnp.logical_or(i == 0, be_ref[i] != prev)

    @pl.when(jnp.logical_and(fresh, i < nu_ref[0]))
    def _():
        wgb[...] = wg_ref[0].astype(bf16)
        wub[...] = wu_ref[0].astype(bf16)
        wdb[...] = wd_ref[0].astype(bf16)

    @pl.when(i < nu_ref[0])
    def _():
        x = xs_ref[...].astype(bf16)
        a = (_silu(_dot(x, wgb[...])) * _dot(x, wub[...])).astype(bf16)
        o_ref[...] = _dot(a, wdb[...])

    @pl.when(i >= nu_ref[0])
    def _():
        o_ref[...] = jnp.zeros_like(o_ref)


def _experts(block_expert, n_used, xs, wg, wu, wd):
    n_rows, d = xs.shape
    ff = wg.shape[2]
    blk = EXPERT_BLOCK
    n_blocks = n_rows // blk

    def row_map(i, be, nu):
        return (jnp.minimum(i, nu[0] - 1), 0)

    def w_map(i, be, nu):
        return (be[i], 0, 0)

    grid_spec = pltpu.PrefetchScalarGridSpec(
        num_scalar_prefetch=2,
        grid=(n_blocks,),
        in_specs=[pl.BlockSpec((blk, d), row_map),
                  pl.BlockSpec((1, d, ff), w_map), pl.BlockSpec((1, d, ff), w_map), pl.BlockSpec((1, ff, d), w_map)],
        out_specs=pl.BlockSpec((blk, d), lambda i, be, nu: (i, 0)),
        scratch_shapes=[pltpu.VMEM((d, ff), bf16), pltpu.VMEM((d, ff), bf16), pltpu.VMEM((ff, d), bf16)],
    )
    return pl.pallas_call(
        _expert_kernel,
        grid_spec=grid_spec,
        out_shape=jax.ShapeDtypeStruct((n_rows, d), f32),
        compiler_params=_params(1, 52),
        name="expert_mlp",
    )(block_expert, n_used, xs, wg, wu, wd)


def _combine_kernel(dest_ref, hp_ref, x1_ref, gt_ref, wt_ref, eo_ref, sg_ref, su_ref, sd_ref, y_ref, rows, sem):
    tm = hp_ref.shape[0]

    def row_copy(r, k):
        return pltpu.make_async_copy(eo_ref.at[pl.ds(dest_ref[k, r], 1)], rows.at[k, pl.ds(r, 1)], sem)

    def issue(r, c):
        for k in range(TOP_K):
            row_copy(r, k).start()
        return c

    lax.fori_loop(0, tm, issue, 0)

    x = hp_ref[...].astype(bf16)
    a = (_silu(_dot(x, sg_ref[...])) * _dot(x, su_ref[...])).astype(bf16)
    y = _dot(a, sd_ref[...])

    def drain(r, c):
        for k in range(TOP_K):
            row_copy(r, k).wait()
        return c

    lax.fori_loop(0, tm, drain, 0)

    wt = wt_ref[...]
    for k in range(TOP_K):
        y = y + wt[:, k:k + 1] * rows[k]
    y_ref[...] = x1_ref[...] + gt_ref[0] * y


def _combine(dest, hp, x1, gate, wt_rows, eo, sg, su, sd, rows_per_mod):
    t, d = x1.shape
    tm = min(COMBINE_TILE, t)
    mod_spec = _mod_spec(gate.shape[1], tm, d, rows_per_mod)
    return pl.pallas_call(
        _combine_kernel,
        grid=(t // tm,),
        in_specs=[pl.BlockSpec((TOP_K, tm), lambda i: (0, i), memory_space=pltpu.SMEM),
                  pl.BlockSpec((tm, d), lambda i: (i, 0)),
                  pl.BlockSpec((tm, d), lambda i: (i, 0)),
                  mod_spec,
                  pl.BlockSpec((tm, TOP_K), lambda i: (i, 0)),
                  pl.BlockSpec(memory_space=pl.ANY),
                  _const_spec(sg.shape), _const_spec(su.shape), _const_spec(sd.shape)],
        out_specs=pl.BlockSpec((tm, d), lambda i: (i, 0)),
        out_shape=jax.ShapeDtypeStruct((t, d), f32),
        scratch_shapes=[pltpu.VMEM((TOP_K, tm, d), f32), pltpu.SemaphoreType.DMA(())],
        compiler_params=_params(1, 48),
        name="combine_shared",
    )(dest, hp, x1, gate, wt_rows, eo, sg, su, sd)


def _moe(hp, idx, wt, rank, counts, x1, gate, wg, wu, wd, sg, su, sd, rows_per_mod, first_expert):
    t = x1.shape[0]
    n_exp = counts.shape[0]
    blk = EXPERT_BLOCK
    n_blocks = -(-(t * TOP_K) // blk) + n_exp
    cnt = counts[:, 0].astype(i32)
    padded = (cnt + blk - 1) // blk * blk
    pad_end = jnp.cumsum(padded)
    pad_start = pad_end - padded
    dest = pad_start[idx] + rank
    block_expert = jnp.minimum(jnp.searchsorted(pad_end, jnp.arange(n_blocks, dtype=i32) * blk, side="right"),
                               n_exp - 1).astype(i32) + first_expert
    n_used = (pad_end[-1:] // blk).astype(i32)
    xs = _dispatch(dest, hp, n_blocks * blk)
    eo = _experts(block_expert, n_used, xs, wg, wu, wd)
    return _combine(dest, hp, x1, gate, wt.T, eo, sg, su, sd, rows_per_mod)


def kernel(x_prompt, x_sample, cache_mla_ckv, cache_mla_kpe, cache_fox_k, cache_fox_v, cache_fox_logf, page_table, c_prompt, c_sample, w_ada, b_ada, g_norm_attn, g_norm_moe, w_in, g_q_a, w_q_b, g_q_mla, g_ckv, g_kpe, w_uk, w_uv, g_q_fox, g_k_fox, b_forget, w_o, w_router, b_router, w_exp_gate, w_exp_up, w_exp_down, w_sh_gate, w_sh_up, w_sh_down):
    depth = w_in.shape[0]
    b, s, d = x_prompt.shape
    bd, ds = x_sample.shape[:2]
    assert ds == 1, "the sample group decodes one token per sequence"
    n_pool, page = cache_mla_ckv.shape[1:3]
    n_pages = page_table.shape[1]
    n_fox, hd_fox = cache_fox_k.shape[3:]
    assert hd_fox == HEAD
    fw = n_fox * HEAD
    kr = cache_mla_ckv.shape[3]
    n_mla = w_uk.shape[2]
    hw = n_mla * HEAD
    n_exp = w_router.shape[2]
    assert n_exp % N_GROUPS == 0 and n_exp // N_GROUPS == 8

    cos_p, sin_p = _rope_tables(jnp.arange(s))
    cos_s, sin_s = _rope_tables(jnp.full((1,), n_pages * page))

    xp = x_prompt.reshape(b * s, d)
    xs = x_sample.reshape(bd, d)
    ff = w_exp_gate.shape[3]
    wg_all = w_exp_gate.reshape(depth * n_exp, d, ff)
    wu_all = w_exp_up.reshape(depth * n_exp, d, ff)
    wd_all = w_exp_down.reshape(depth * n_exp, ff, d)
    ckv_all = cache_mla_ckv.reshape(depth * n_pool, page, kr)
    kpe_all = cache_mla_kpe.reshape(depth * n_pool, page, ROPE)
    fk_all = cache_fox_k.reshape(depth * n_pool, page, fw)
    fv_all = cache_fox_v.reshape(depth * n_pool, page, fw)
    lf_all = cache_fox_logf.reshape(depth * n_pool, page, n_fox)
    st = [[] for _ in range(10)]
    for l in range(depth):
        pw = _pack_proj_weights(w_in[l], g_q_a[l], w_q_b[l], g_q_mla[l], g_ckv[l], g_kpe[l], w_uk[l], w_uv[l],
                                g_q_fox[l], g_k_fox[l], b_forget[l])
        w_o_a = w_o[l][:hw].astype(bf16)
        w_o_b = w_o[l][hw:].astype(bf16)
        w_router_t = w_router[l].T
        b_router_c = b_router[l].reshape(n_exp, 1)
        g_attn = g_norm_attn[l].reshape(1, d)
        g_moe = g_norm_moe[l].reshape(1, d)
        sg, su, sd = w_sh_gate[l].astype(bf16), w_sh_up[l].astype(bf16), w_sh_down[l].astype(bf16)
        moe_w = (wg_all, wu_all, wd_all, sg, su, sd)

        mod = _modulation(jnp.concatenate([c_prompt, c_sample], axis=0), w_ada[l], b_ada[l])
        mod_p = [m.reshape(b, 1, d) for m in jnp.split(mod[:b], 6, axis=-1)]
        mod_s = [m.reshape(1, bd, d) for m in jnp.split(mod[b:], 6, axis=-1)]

        sh1, sc1, gt1, sh2, sc2, gt2 = mod_p
        (ckv, kpe, kf, vf, logf, qm, km, vm, qfb, kfb, vfb) = _project(
            xp, sh1, sc1, g_attn, pw, cos_p, sin_p, rows_per_mod=s, rows_per_pos=s)
        cum = _cumsum_seq(logf.reshape(b, s, n_fox))
        o_mla = _flash(qm, km, vm, b, s, n_mla, MLA_QK, HEAD)
        o_fox = _flash(qfb, kfb, vfb, b, s, n_fox, HEAD, HEAD, cum=cum)
        x1, hp, idx, wt, rank, counts = _oproj_router(o_mla, o_fox, w_o_a, w_o_b, xp, gt1, sh2, sc2, g_moe,
                                                      w_router_t, b_router_c, rows_per_mod=s)
        xp = _moe(hp, idx, wt, rank, counts, x1, gt2, *moe_w, rows_per_mod=s, first_expert=l * n_exp)
        for lst, arr in zip(st[:5], (ckv.reshape(b, s, kr), kpe.reshape(b, s, ROPE), kf.reshape(b, s, n_fox, HEAD),
                                     vf.reshape(b, s, n_fox, HEAD), logf.reshape(b, s, n_fox))):
            lst.append(arr)

        sh1, sc1, gt1, sh2, sc2, gt2 = mod_s
        (ckv, kpe, kf, vf, logf, qm, km, vm, qfb, kfb, vfb) = _project(
            xs, sh1, sc1, g_attn, pw, cos_s, sin_s, rows_per_mod=bd, rows_per_pos=bd)
        q_lat = _absorb(qm, pw["w_uk"], n_mla).reshape(bd, n_mla, kr)
        q_pe = qm.reshape(bd, n_mla, MLA_QK)[:, :, NOPE:]
        o_lat, o_fox = _decode(page_table + l * n_pool, q_lat, q_pe, qfb.reshape(bd, 1, fw),
                               logf.reshape(bd, n_fox, 1), ckv.reshape(bd, 1, kr), kpe.reshape(bd, 1, ROPE),
                               kf.reshape(bd, 1, fw), vf.reshape(bd, 1, fw),
                               ckv_all, kpe_all, fk_all, fv_all, lf_all)
        mix_mla, mix_fox = _expand(o_lat.reshape(bd, n_mla * kr), o_fox.reshape(bd, fw), pw["w_uv"], n_mla)
        x1, hp, idx, wt, rank, counts = _oproj_router(mix_mla, mix_fox, w_o_a, w_o_b, xs, gt1, sh2, sc2, g_moe,
                                                      w_router_t, b_router_c, rows_per_mod=bd)
        xs = _moe(hp, idx, wt, rank, counts, x1, gt2, *moe_w, rows_per_mod=bd, first_expert=l * n_exp)
        for lst, arr in zip(st[5:], (ckv.reshape(bd, 1, kr), kpe.reshape(bd, 1, ROPE), kf.reshape(bd, 1, n_fox, HEAD),
                                     vf.reshape(bd, 1, n_fox, HEAD), logf.reshape(bd, 1, n_fox))):
            lst.append(arr)

    stacked = [jnp.stack(a, 0) for a in st]
    return (xp.reshape(b, s, d), xs.reshape(bd, 1, d), *stacked)
```

```python
import functools

import jax
import jax.numpy as jnp
from jax import lax
from jax.experimental import pallas as pl
from jax.experimental.pallas import tpu as pltpu

f32 = jnp.float32
bf16 = jnp.bfloat16
i32 = jnp.int32
u32 = jnp.uint32

LANES = 128
NOPE = 128
ROPE = 64
HEAD = 128
MLA_QK = 2 * LANES
MLA_SCALE = (NOPE + ROPE) ** -0.5
FOX_SCALE = HEAD ** -0.5
ROPE_THETA = 10000.0
N_GROUPS = 8
TOPK_GROUPS = 4
TOP_K = 8
ROUTED_SCALE = 2.5
NORM_EPS = 1e-6
NEG_INF = float("-inf")

ROW_TILE = 256
ATT_TILE = 512
EXPERT_BLOCK = 256
COMBINE_TILE = 128
DECODE_PAGES = 8
MOD_TILE = 1024


def _params(n_axes, vmem_mb):
    return pltpu.CompilerParams(dimension_semantics=("arbitrary",) * n_axes,
                                vmem_limit_bytes=vmem_mb << 20)


def _const_spec(shape):
    nd = len(shape)
    return pl.BlockSpec(shape, lambda *_: (0,) * nd, pipeline_mode=pl.Buffered(1))


def _mod_spec(mod_rows, tm, d, rows_per_mod):
    if mod_rows == 1:
        tiles_per_mod = rows_per_mod // tm
        return pl.BlockSpec((1, 1, d), lambda i: (i // tiles_per_mod, 0, 0))
    return pl.BlockSpec((1, tm, d), lambda i: (0, i, 0))


def _sigmoid(x):
    return 1.0 / (1.0 + jnp.exp(-x))


def _silu(x):
    return x * _sigmoid(x)


def _rms(x, width):
    return lax.rsqrt(jnp.sum(x * x, axis=-1, keepdims=True) * (1.0 / width) + NORM_EPS)


def _dot(a, b):
    return jnp.dot(a, b, preferred_element_type=f32)


def _dot_nt(a, b):
    return lax.dot_general(a, b, (((1,), (1,)), ((), ())), preferred_element_type=f32)


def _floor_div(x, n):
    if n & (n - 1) == 0:
        return jnp.right_shift(x, n.bit_length() - 1)
    return x // n


def _split3(x):
    hi = x.astype(bf16)
    r = x - hi.astype(f32)
    mid = r.astype(bf16)
    lo = (r - mid.astype(f32)).astype(bf16)
    return hi, mid, lo


def _mod_kernel(c_ref, w_ref, b_ref, o_ref):
    a = _silu(c_ref[...]).astype(bf16)
    o_ref[...] = _dot(a, w_ref[...].astype(bf16)) + b_ref[...]


def _modulation(c, w_ada, b_ada):
    m, d = c.shape
    n = w_ada.shape[1]
    tn = min(MOD_TILE, n)
    return pl.pallas_call(
        _mod_kernel,
        grid=(n // tn,),
        in_specs=[pl.BlockSpec((m, d), lambda j: (0, 0)),
                  pl.BlockSpec((d, tn), lambda j: (0, j)),
                  pl.BlockSpec((1, tn), lambda j: (0, j))],
        out_specs=pl.BlockSpec((m, tn), lambda j: (0, j)),
        out_shape=jax.ShapeDtypeStruct((m, n), f32),
        compiler_params=_params(1, 40),
        name="modulation",
    )(c, w_ada, b_ada.reshape(1, n))


def _proj_kernel(x_ref, sh_ref, sc_ref, gn_ref, win_ref, gqa_ref, wqb_ref, gckv_ref, wuk_ref, wuv_ref,
                 small_ref, cos_ref, sin_ref,
                 ckv_ref, kpe_ref, kf_ref, vf_ref, logf_ref,
                 qm_ref, km_ref, vm_ref, qfb_ref, kfb_ref, vfb_ref, *, dims):
    d, qr, kr, n_mla, n_fox = dims
    fw = n_fox * HEAD
    x = x_ref[...]
    h = x * _rms(x, d) * gn_ref[...]
    h = h * (1.0 + sc_ref[0]) + sh_ref[0]
    hb = h.astype(bf16)
    cos = cos_ref[...]
    sin = sin_ref[...]
    g_qn, g_pea, g_peb = small_ref[0:1, :], small_ref[1:2, :], small_ref[2:3, :]
    g_ka, g_kb = small_ref[3:4, :], small_ref[4:5, :]
    g_qf, g_kf, b_f = small_ref[5:6, :], small_ref[6:7, :], small_ref[7:8, :]

    o_ckv = qr
    o_qf = o_ckv + kr
    o_kf = o_qf + fw
    o_vf = o_kf + fw
    o_ka = o_vf + fw
    o_kb = o_ka + LANES
    o_f = o_kb + LANES

    ka = _dot(hb, win_ref[:, o_ka:o_ka + LANES])
    kb = _dot(hb, win_ref[:, o_kb:o_kb + LANES])
    r = _rms(ka, ROPE)
    kpe = (ka * r * g_ka) * cos + (kb * r * g_kb) * sin
    kpe_ref[...] = kpe[:, :ROPE]
    kpe_b = kpe.astype(bf16)

    c = _dot(hb, win_ref[:, o_ckv:o_ckv + kr])
    cn = c * _rms(c, kr) * gckv_ref[...]
    ckv_ref[...] = cn
    cb = cn.astype(bf16)
    k_nope = _dot(cb, wuk_ref[...])
    vm_ref[...] = _dot(cb, wuv_ref[...]).astype(bf16)

    qa = _dot(hb, win_ref[:, 0:qr])
    qa = (qa * _rms(qa, qr) * gqa_ref[...]).astype(bf16)
    hw = n_mla * LANES
    for hd in range(n_mla):
        lo = hd * LANES
        qn = _dot(qa, wqb_ref[:, lo:lo + LANES])
        pa = _dot(qa, wqb_ref[:, hw + lo:hw + lo + LANES])
        pb = _dot(qa, wqb_ref[:, 2 * hw + lo:2 * hw + lo + LANES])
        ss = jnp.sum(qn * qn, axis=-1, keepdims=True) + jnp.sum(pa * pa, axis=-1, keepdims=True)
        r = lax.rsqrt(ss * (1.0 / (NOPE + ROPE)) + NORM_EPS)
        q_nope = qn * r * g_qn
        q_pe = (pa * r * g_pea) * cos + (pb * r * g_peb) * sin
        base = hd * MLA_QK
        qm_ref[:, base:base + LANES] = (q_nope * MLA_SCALE).astype(bf16)
        qm_ref[:, base + LANES:base + MLA_QK] = (q_pe * MLA_SCALE).astype(bf16)
        km_ref[:, base:base + LANES] = k_nope[:, lo:lo + LANES].astype(bf16)
        km_ref[:, base + LANES:base + MLA_QK] = kpe_b

    for hd in range(n_fox):
        lo = hd * HEAD
        qf = _dot(hb, win_ref[:, o_qf + lo:o_qf + lo + HEAD])
        qfb_ref[:, lo:lo + HEAD] = (qf * _rms(qf, HEAD) * g_qf * FOX_SCALE).astype(bf16)
        kf = _dot(hb, win_ref[:, o_kf + lo:o_kf + lo + HEAD])
        kf = kf * _rms(kf, HEAD) * g_kf
        kf_ref[:, lo:lo + HEAD] = kf
        kfb_ref[:, lo:lo + HEAD] = kf.astype(bf16)
    vf = _dot(hb, win_ref[:, o_vf:o_vf + fw])
    vf_ref[...] = vf
    vfb_ref[...] = vf.astype(bf16)
    z = _dot(hb, win_ref[:, o_f:o_f + LANES]) + b_f
    logf = jnp.minimum(z, 0.0) - jnp.log1p(jnp.exp(-jnp.abs(z)))
    logf_ref[...] = logf[:, :n_fox]


def _project(x, shift, scale, g_norm, pw, cos_t, sin_t, rows_per_mod, rows_per_pos):
    t, d = x.shape
    qr, kr, n_mla, n_fox = pw["dims"]
    fw = n_fox * HEAD
    tm = min(ROW_TILE, t)
    tiles_per_pos = max(rows_per_pos // tm, 1)
    pos_rows = cos_t.shape[0]
    tp = min(tm, pos_rows)
    mod_spec = _mod_spec(shift.shape[1], tm, d, rows_per_mod)
    pos_spec = pl.BlockSpec((tp, LANES), lambda i: (i % tiles_per_pos, 0))

    def row_spec(w):
        return pl.BlockSpec((tm, w), lambda i: (i, 0))

    out_widths = [(kr, f32), (ROPE, f32), (fw, f32), (fw, f32), (n_fox, f32),
                  (n_mla * MLA_QK, bf16), (n_mla * MLA_QK, bf16), (n_mla * HEAD, bf16),
                  (fw, bf16), (fw, bf16), (fw, bf16)]
    return pl.pallas_call(
        functools.partial(_proj_kernel, dims=(d, qr, kr, n_mla, n_fox)),
        grid=(t // tm,),
        in_specs=[row_spec(d), mod_spec, mod_spec, _const_spec((1, d)),
                  _const_spec(pw["w_in"].shape), _const_spec((1, qr)), _const_spec(pw["w_qb"].shape),
                  _const_spec((1, kr)), _const_spec(pw["w_uk"].shape), _const_spec(pw["w_uv"].shape),
                  _const_spec((8, LANES)), pos_spec, pos_spec],
        out_specs=[row_spec(w) for w, _ in out_widths],
        out_shape=[jax.ShapeDtypeStruct((t, w), dt) for w, dt in out_widths],
        compiler_params=_params(1, 56),
        name="project",
    )(x, shift, scale, g_norm, pw["w_in"], pw["g_qa"], pw["w_qb"], pw["g_ckv"], pw["w_uk"], pw["w_uv"],
      pw["small"], cos_t, sin_t)


def _pad_lanes(a):
    return jnp.pad(a, [(0, 0)] * (a.ndim - 1) + [(0, LANES - a.shape[-1])])


def _rot_half(a):
    half = a.shape[-1] // 2
    return jnp.concatenate([a[..., half:], a[..., :half]], axis=-1)


def _pack_proj_weights(w_in, g_q_a, w_q_b, g_q_mla, g_ckv, g_kpe, w_uk, w_uv, g_q_fox, g_k_fox, b_forget):
    qr = g_q_a.shape[0]
    kr = g_ckv.shape[0]
    n_mla = w_uk.shape[1]
    n_fox = b_forget.shape[0]
    fw = n_fox * HEAD
    o = 0
    w_qa = w_in[:, o:o + qr]; o += qr
    w_ckv = w_in[:, o:o + kr]; o += kr
    w_kpe = w_in[:, o:o + ROPE]; o += ROPE
    w_fox = w_in[:, o:o + 3 * fw]; o += 3 * fw
    w_f = w_in[:, o:o + n_fox]
    w_in_p = jnp.concatenate([w_qa, w_ckv, w_fox, _pad_lanes(w_kpe), _pad_lanes(_rot_half(w_kpe)),
                              _pad_lanes(w_f)], axis=1).astype(bf16)
    wq = w_q_b.reshape(qr, n_mla, NOPE + ROPE)
    w_nope = wq[:, :, :NOPE].reshape(qr, n_mla * NOPE)
    w_pe = wq[:, :, NOPE:]
    w_pea = _pad_lanes(w_pe).reshape(qr, n_mla * LANES)
    w_peb = _pad_lanes(_rot_half(w_pe)).reshape(qr, n_mla * LANES)
    w_qb_p = jnp.concatenate([w_nope, w_pea, w_peb], axis=1).astype(bf16)
    g_pe = g_q_mla[NOPE:]
    small = jnp.stack([g_q_mla[:NOPE], _pad_lanes(g_pe), _pad_lanes(_rot_half(g_pe)),
                       _pad_lanes(g_kpe), _pad_lanes(_rot_half(g_kpe)),
                       g_q_fox, g_k_fox, _pad_lanes(b_forget)]).astype(f32)
    return {"dims": (qr, kr, n_mla, n_fox), "w_in": w_in_p, "g_qa": g_q_a.reshape(1, qr), "w_qb": w_qb_p,
            "g_ckv": g_ckv.reshape(1, kr), "w_uk": w_uk.reshape(kr, n_mla * NOPE).astype(bf16),
            "w_uv": w_uv.reshape(kr, n_mla * HEAD).astype(bf16), "small": small}


def _rope_tables(pos):
    half = ROPE // 2
    inv = ROPE_THETA ** (-jnp.arange(half, dtype=f32) / half)
    ang = pos.astype(f32)[:, None] * inv[None, :]
    cos, sin = jnp.cos(ang), jnp.sin(ang)
    return _pad_lanes(jnp.concatenate([cos, cos], axis=1)), _pad_lanes(jnp.concatenate([-sin, sin], axis=1))


def _cumsum_kernel(x_ref, o_ref):
    s = x_ref.shape[1]
    x = x_ref[0]
    tri = (lax.broadcasted_iota(i32, (s, s), 1) <= lax.broadcasted_iota(i32, (s, s), 0)).astype(bf16)
    hi, mid, lo = _split3(x)
    o_ref[0] = _dot(tri, hi) + _dot(tri, mid) + _dot(tri, lo)


def _cumsum_seq(x):
    b, s, h = x.shape
    return pl.pallas_call(
        _cumsum_kernel,
        grid=(b,),
        in_specs=[pl.BlockSpec((1, s, h), lambda i: (i, 0, 0))],
        out_specs=pl.BlockSpec((1, s, h), lambda i: (i, 0, 0)),
        out_shape=jax.ShapeDtypeStruct((b, s, h), f32),
        compiler_params=_params(1, 40),
        name="cumsum_logf",
    )(x)


def _flash_kernel(*refs, has_bias):
    if has_bias:
        q_ref, k_ref, v_ref, cq_ref, ck_ref, o_ref = refs
    else:
        q_ref, k_ref, v_ref, o_ref = refs
    tq = q_ref.shape[0]
    dv = v_ref.shape[1]
    i = pl.program_id(2)
    q = q_ref[...]

    def step(j, carry, diagonal):
        m, l, acc = carry
        start = pl.multiple_of(j * tq, tq)
        s = _dot_nt(q, k_ref[pl.ds(start, tq), :])
        if has_bias:
            s = s + (cq_ref[0, 0] - ck_ref[0, 0, pl.ds(j, 1), :])
        if diagonal:
            keep = lax.broadcasted_iota(i32, (tq, tq), 1) <= lax.broadcasted_iota(i32, (tq, tq), 0)
            s = jnp.where(keep, s, NEG_INF)
        m_new = jnp.maximum(m, jnp.max(s, axis=-1, keepdims=True))
        p = jnp.exp(s - m_new)
        corr = jnp.exp(m - m_new)
        l = l * corr + jnp.sum(p, axis=-1, keepdims=True)
        acc = acc * corr + _dot(p.astype(bf16), v_ref[pl.ds(start, tq), :])
        return m_new, l, acc

    init = (jnp.full((tq, 1), NEG_INF, f32), jnp.zeros((tq, 1), f32), jnp.zeros((tq, dv), f32))
    carry = step(i, init, True)
    m, l, acc = lax.fori_loop(0, i, lambda j, c: step(j, c, False), carry)
    o_ref[...] = (acc / l).astype(o_ref.dtype)


def _flash(q, k, v, batch, seq, n_heads, dk, dv, cum=None):
    t = q.shape[0]
    tq = min(ATT_TILE, seq)
    nq = seq // tq
    in_specs = [pl.BlockSpec((tq, dk), lambda b, h, i: (b * nq + i, h)),
                pl.BlockSpec((seq, dk), lambda b, h, i: (b, h)),
                pl.BlockSpec((seq, dv), lambda b, h, i: (b, h))]
    args = [q, k, v]
    if cum is not None:
        cum_t = cum.transpose(0, 2, 1)
        in_specs += [pl.BlockSpec((1, 1, tq, 1), lambda b, h, i: (b, h, i, 0)),
                     pl.BlockSpec((1, 1, nq, tq), lambda b, h, i: (b, h, 0, 0))]
        args += [cum_t.reshape(batch, n_heads, seq, 1), cum_t.reshape(batch, n_heads, nq, tq)]
    return pl.pallas_call(
        functools.partial(_flash_kernel, has_bias=cum is not None),
        grid=(batch, n_heads, nq),
        in_specs=in_specs,
        out_specs=pl.BlockSpec((tq, dv), lambda b, h, i: (b * nq + i, h)),
        out_shape=jax.ShapeDtypeStruct((t, n_heads * dv), bf16),
        compiler_params=_params(3, 40),
        name="flash_bias" if cum is not None else "flash",
    )(*args)


def _absorb_kernel(q_ref, wuk_ref, o_ref, *, n_mla):
    kr = wuk_ref.shape[0]
    for hd in range(n_mla):
        qn = q_ref[:, hd * MLA_QK:hd * MLA_QK + NOPE]
        o_ref[:, hd * kr:(hd + 1) * kr] = _dot_nt(qn, wuk_ref[:, hd * NOPE:(hd + 1) * NOPE]).astype(bf16)


def _absorb(qm, w_uk2, n_mla):
    bd = qm.shape[0]
    kr = w_uk2.shape[0]
    return pl.pallas_call(
        functools.partial(_absorb_kernel, n_mla=n_mla),
        out_shape=jax.ShapeDtypeStruct((bd, n_mla * kr), bf16),
        name="absorb_query",
    )(qm, w_uk2)


def _expand_kernel(olat_ref, ofox_ref, wuv_ref, mla_ref, fox_ref, *, n_mla):
    kr = wuv_ref.shape[0]
    for hd in range(n_mla):
        o = olat_ref[:, hd * kr:(hd + 1) * kr].astype(bf16)
        mla_ref[:, hd * HEAD:(hd + 1) * HEAD] = _dot(o, wuv_ref[:, hd * HEAD:(hd + 1) * HEAD]).astype(bf16)
    fox_ref[...] = ofox_ref[...].astype(bf16)


def _expand(o_lat, o_fox, w_uv2, n_mla):
    bd = o_lat.shape[0]
    return pl.pallas_call(
        functools.partial(_expand_kernel, n_mla=n_mla),
        out_shape=[jax.ShapeDtypeStruct((bd, n_mla * HEAD), bf16),
                   jax.ShapeDtypeStruct(o_fox.shape, bf16)],
        name="expand_latent",
    )(o_lat, o_fox, w_uv2)


def _decode_kernel(pt_ref, qlat_ref, qpe_ref, qf_ref, fnew_ref, ckvn_ref, kpen_ref, kfn_ref, vfn_ref, *rest,
                   n_pages_step, n_fox):
    g = n_pages_step
    ckv_refs, kpe_refs = rest[0:g], rest[g:2 * g]
    fk_refs, fv_refs, lf_refs = rest[2 * g:3 * g], rest[3 * g:4 * g], rest[4 * g:5 * g]
    olat_ref, ofox_ref = rest[5 * g], rest[5 * g + 1]
    m_m, l_m, acc_m, m_f, l_f, acc_f, suf = rest[5 * g + 2:]
    j = pl.program_id(1)
    page = ckv_refs[0].shape[1]
    rows = page * n_fox

    q_lat = qlat_ref[0]
    q_pe = qpe_ref[0][:, :ROPE]
    q_fox = qf_ref[0]
    f_new = fnew_ref[0]

    @pl.when(j == 0)
    def _():
        ckv_n = ckvn_ref[0].astype(bf16).astype(f32)
        kpe_n = kpen_ref[0].astype(bf16).astype(f32)
        s_m = (jnp.sum(q_lat.astype(f32) * ckv_n, axis=-1, keepdims=True)
               + jnp.sum(q_pe.astype(f32) * kpe_n, axis=-1, keepdims=True))
        m_m[...] = s_m
        l_m[...] = jnp.ones_like(s_m)
        acc_m[...] = jnp.broadcast_to(ckv_n, acc_m.shape)
        kf_n = kfn_ref[0].astype(bf16).astype(f32)
        s_f = jnp.sum(q_fox.astype(f32) * kf_n, axis=-1, keepdims=True)
        m_f[...] = s_f
        l_f[...] = jnp.ones_like(s_f)
        acc_f[...] = vfn_ref[0].astype(bf16).astype(f32)
        suf[...] = jnp.zeros_like(suf)

    col = lax.broadcasted_iota(i32, (n_fox, rows), 1)
    own = (col - _floor_div(col, n_fox) * n_fox) == lax.broadcasted_iota(i32, (n_fox, rows), 0)
    later = (lax.broadcasted_iota(i32, (page, rows), 0)
             > _floor_div(lax.broadcasted_iota(i32, (page, rows), 1), n_fox)).astype(bf16)
    suffix = suf[...]
    s_mla, s_fox, ckv_b, fv_b = [], [], [], []
    for p in range(g):
        cb = ckv_refs[p][0].astype(bf16)
        ckv_b.append(cb)
        s_mla.append(_dot_nt(q_lat, cb) + _dot(q_pe, kpe_refs[p][0].astype(bf16)))
        lf = lf_refs[p][0]
        hi, mid, lo = _split3(lf)
        parts = jnp.concatenate([hi.astype(f32), mid.astype(f32), lo.astype(f32)], axis=0).astype(bf16)
        sums = _dot(parts, later)
        after = sums[0:n_fox] + sums[n_fox:2 * n_fox] + sums[2 * n_fox:3 * n_fox]
        s = _dot_nt(q_fox, fk_refs[p][0].astype(bf16)) + (after + (suffix + f_new))
        s_fox.append(jnp.where(own, s, NEG_INF))
        suffix = suffix + jnp.sum(lf, axis=-1, keepdims=True)
        fv_b.append(fv_refs[p][0].astype(bf16))
    suf[...] = suffix

    def update(s_list, vals, m_ref, l_ref, acc_ref):
        width = s_list[0].shape[1]
        s = jnp.concatenate(s_list, axis=1)
        m_old = m_ref[...]
        m_new = jnp.maximum(m_old, jnp.max(s, axis=-1, keepdims=True))
        corr = jnp.exp(m_old - m_new)
        pr = jnp.exp(s - m_new)
        l_ref[...] = l_ref[...] * corr + jnp.sum(pr, axis=-1, keepdims=True)
        pb = pr.astype(bf16)
        acc = acc_ref[...] * corr
        for p in range(g):
            acc = acc + _dot(pb[:, p * width:(p + 1) * width], vals[p])
        acc_ref[...] = acc
        m_ref[...] = m_new

    update(s_mla, ckv_b, m_m, l_m, acc_m)
    update(s_fox, fv_b, m_f, l_f, acc_f)

    @pl.when(j == pl.num_programs(1) - 1)
    def _():
        olat_ref[0] = acc_m[...] / l_m[...]
        ofox_ref[0] = acc_f[...] / l_f[...]


def _decode(page_table, q_lat, q_pe, q_fox, f_new, ckv_new, kpe_new, kf_new, vf_new,
            cache_ckv, cache_kpe, cache_fk, cache_fv, cache_logf):
    bd, n_pages = page_table.shape
    n_mla, kr = q_lat.shape[1:]
    n_fox = q_fox.shape[1]
    page = cache_ckv.shape[1]
    g = min(DECODE_PAGES, n_pages)
    steps = n_pages // g

    def per_batch(shape):
        nd = len(shape)
        return pl.BlockSpec((1,) + tuple(shape[1:]), lambda b, j, pt: (b,) + (0,) * (nd - 1))

    def paged(shape, p):
        return pl.BlockSpec((1,) + tuple(shape[1:]), lambda b, j, pt: (pt[b, n_pages - 1 - (j * g + p)], 0, 0))

    small_in = [q_lat, q_pe, q_fox, f_new, ckv_new, kpe_new, kf_new, vf_new]
    in_specs = [per_batch(a.shape) for a in small_in]
    caches = []
    for arr in (cache_ckv, cache_kpe, cache_fk, cache_fv, cache_logf):
        in_specs += [paged(arr.shape, p) for p in range(g)]
        caches += [arr] * g
    grid_spec = pltpu.PrefetchScalarGridSpec(
        num_scalar_prefetch=1,
        grid=(bd, steps),
        in_specs=in_specs,
        out_specs=[pl.BlockSpec((1, n_mla, kr), lambda b, j, pt: (b, 0, 0)),
                   pl.BlockSpec((1, n_fox, HEAD), lambda b, j, pt: (b, 0, 0))],
        scratch_shapes=[pltpu.VMEM((n_mla, 1), f32), pltpu.VMEM((n_mla, 1), f32), pltpu.VMEM((n_mla, kr), f32),
                        pltpu.VMEM((n_fox, 1), f32), pltpu.VMEM((n_fox, 1), f32), pltpu.VMEM((n_fox, HEAD), f32),
                        pltpu.VMEM((n_fox, 1), f32)],
    )
    return pl.pallas_call(
        functools.partial(_decode_kernel, n_pages_step=g, n_fox=n_fox),
        grid_spec=grid_spec,
        out_shape=[jax.ShapeDtypeStruct((bd, n_mla, kr), f32), jax.ShapeDtypeStruct((bd, n_fox, HEAD), f32)],
        compiler_params=_params(2, 48),
        name="paged_decode",
    )(page_table, *small_in, *caches)


def _pack_pairs(h):
    half = h.shape[1] // 2
    hb = h.astype(bf16).astype(f32)
    top = pltpu.bitcast(hb[:, :half], u32)
    bot = pltpu.bitcast(hb[:, half:], u32)
    return top | (bot >> 16)


def _unpack_pairs(w):
    top = pltpu.bitcast(w & jnp.uint32(0xFFFF0000), f32)
    bot = pltpu.bitcast(w << 16, f32)
    return top, bot


def _unpack_bf16(w):
    top, bot = _unpack_pairs(w)
    return jnp.concatenate([top, bot], axis=1).astype(bf16)


def _oproj_kernel(mla_ref, fox_ref, woa_ref, wob_ref, x_ref, gt_ref, sh_ref, sc_ref, gn_ref, wr_ref, br_ref,
                  cin_ref, x1_ref, hp_ref, idx_ref, wt_ref, rank_ref, cnt_ref, carry_ref):
    d = x_ref.shape[1]
    tm = x_ref.shape[0]
    n_exp = wr_ref.shape[0]
    per_group = n_exp // N_GROUPS

    @pl.when(pl.program_id(0) == 0)
    def _():
        carry_ref[...] = cin_ref[...]

    mix = _dot(mla_ref[...], woa_ref[...]) + _dot(fox_ref[...], wob_ref[...])
    x1 = x_ref[...] + gt_ref[0] * mix
    x1_ref[...] = x1
    h = x1 * _rms(x1, d) * gn_ref[...]
    h = h * (1.0 + sc_ref[0]) + sh_ref[0]
    hp_ref[...] = _pack_pairs(h)

    h_hi = h.astype(bf16)
    h_lo = (h - h_hi.astype(f32)).astype(bf16)
    wr = wr_ref[...]
    w_hi = wr.astype(bf16)
    w_lo = (wr - w_hi.astype(f32)).astype(bf16)
    logits = _dot_nt(w_hi, h_hi) + (_dot_nt(w_hi, h_lo) + _dot_nt(w_lo, h_hi))
    scores = _sigmoid(logits).reshape(N_GROUPS, per_group, tm)
    sel = scores + br_ref[...].reshape(N_GROUPS, per_group, 1)

    in_group = lax.broadcasted_iota(i32, (N_GROUPS, per_group, tm), 1).astype(f32)
    group_id = lax.broadcasted_iota(i32, (N_GROUPS, per_group, tm), 0).astype(f32)
    expert_id = group_id * per_group + in_group

    m1 = jnp.max(sel, axis=1, keepdims=True)
    first = jnp.min(jnp.where(sel == m1, in_group, float(per_group)), axis=1, keepdims=True)
    m2 = jnp.max(jnp.where(in_group == first, NEG_INF, sel), axis=1, keepdims=True)
    gscore = m1 + m2
    gid = lax.broadcasted_iota(i32, (N_GROUPS, 1, tm), 0).astype(f32)
    gkeep = jnp.zeros((N_GROUPS, 1, tm), f32)
    for _ in range(TOPK_GROUPS):
        best = jnp.max(gscore, axis=0, keepdims=True)
        pick = jnp.min(jnp.where(gscore == best, gid, float(N_GROUPS)), axis=0, keepdims=True)
        hit = gid == pick
        gkeep = jnp.where(hit, 1.0, gkeep)
        gscore = jnp.where(hit, NEG_INF, gscore)

    cand = jnp.where(jnp.broadcast_to(gkeep, sel.shape) > 0.5, sel, NEG_INF)
    chosen = jnp.zeros((N_GROUPS, per_group, tm), f32)
    picks, weights = [], []
    for _ in range(TOP_K):
        best = jnp.max(jnp.max(cand, axis=1, keepdims=True), axis=0, keepdims=True)
        pick = jnp.min(jnp.min(jnp.where(cand == best, expert_id, float(n_exp)), axis=1, keepdims=True),
                       axis=0, keepdims=True)
        hit = expert_id == pick
        weights.append(jnp.sum(jnp.sum(jnp.where(hit, scores, 0.0), axis=1, keepdims=True), axis=0))
        picks.append(pick)
        chosen = jnp.where(hit, 1.0, chosen)
        cand = jnp.where(hit, NEG_INF, cand)

    wsum = weights[0]
    for w in weights[1:]:
        wsum = wsum + w
    chosen2 = chosen.reshape(n_exp, tm)
    upto = (lax.broadcasted_iota(i32, (tm, tm), 0) <= lax.broadcasted_iota(i32, (tm, tm), 1)).astype(bf16)
    incl = _dot(chosen2.astype(bf16), upto)
    before = (carry_ref[...] + (incl - chosen2)).reshape(N_GROUPS, per_group, tm)
    carry_ref[...] = carry_ref[...] + jnp.sum(chosen2, axis=1, keepdims=True)
    cnt_ref[...] = carry_ref[...]
    for k in range(TOP_K):
        hit = expert_id == picks[k]
        rk = jnp.sum(jnp.sum(jnp.where(hit, before, 0.0), axis=1, keepdims=True), axis=0)
        idx_ref[k:k + 1, :] = picks[k][0].astype(i32)
        wt_ref[k:k + 1, :] = weights[k] / wsum * ROUTED_SCALE
        rank_ref[k:k + 1, :] = rk.astype(i32)


def _oproj_router(mla, fox, w_o_a, w_o_b, x, gate, shift, scale, g_norm, w_router_t, b_router, counts_in,
                  rows_per_mod):
    t, d = x.shape
    tm = min(ROW_TILE, t)
    n_exp = w_router_t.shape[0]
    mod_spec = _mod_spec(shift.shape[1], tm, d, rows_per_mod)

    def row_spec(w):
        return pl.BlockSpec((tm, w), lambda i: (i, 0))

    def tok_spec():
        return pl.BlockSpec((TOP_K, tm), lambda i: (0, i))

    return pl.pallas_call(
        _oproj_kernel,
        grid=(t // tm,),
        in_specs=[row_spec(mla.shape[1]), row_spec(fox.shape[1]), _const_spec(w_o_a.shape), _const_spec(w_o_b.shape),
                  row_spec(d), mod_spec, mod_spec, mod_spec, _const_spec((1, d)),
                  _const_spec((n_exp, d)), _const_spec((n_exp, 1)), _const_spec((n_exp, 1))],
        out_specs=[row_spec(d), row_spec(d // 2), tok_spec(), tok_spec(), tok_spec(),
                   pl.BlockSpec((n_exp, 1), lambda i: (0, 0))],
        out_shape=[jax.ShapeDtypeStruct((t, d), f32), jax.ShapeDtypeStruct((t, d // 2), u32),
                   jax.ShapeDtypeStruct((TOP_K, t), i32), jax.ShapeDtypeStruct((TOP_K, t), f32),
                   jax.ShapeDtypeStruct((TOP_K, t), i32), jax.ShapeDtypeStruct((n_exp, 1), f32)],
        scratch_shapes=[pltpu.VMEM((n_exp, 1), f32)],
        compiler_params=_params(1, 48),
        name="oproj_router",
    )(mla, fox, w_o_a, w_o_b, x, gate, shift, scale, g_norm, w_router_t, b_router, counts_in)


def _dispatch_kernel(tail_ref, dest_ref, hp_ref, xs_in_ref, xs_ref, zeros_ref, sem, zsem, *, zero_tails):
    del xs_in_ref
    tm = hp_ref.shape[0]
    blk = zeros_ref.shape[0]

    if zero_tails:
        @pl.when(pl.program_id(0) == 0)
        def _():
            zeros_ref[...] = jnp.zeros_like(zeros_ref)

            def tail_copy(e):
                start = pl.multiple_of(jnp.maximum(tail_ref[e], 0), blk)
                return pltpu.make_async_copy(zeros_ref, xs_ref.at[pl.ds(start, blk)], zsem)

            def issue_tail(e, c):
                @pl.when(tail_ref[e] >= 0)
                def _():
                    tail_copy(e).start()
                return c

            def drain_tail(e, c):
                @pl.when(tail_ref[e] >= 0)
                def _():
                    tail_copy(e).wait()
                return c

            lax.fori_loop(0, tail_ref.shape[0], issue_tail, 0)
            lax.fori_loop(0, tail_ref.shape[0], drain_tail, 0)

    def row_copy(r, k):
        return pltpu.make_async_copy(hp_ref.at[pl.ds(r, 1)], xs_ref.at[pl.ds(dest_ref[k, r], 1)], sem)

    def issue(r, c):
        for k in range(TOP_K):
            row_copy(r, k).start()
        return c

    lax.fori_loop(0, tm, issue, 0)

    def drain(r, c):
        for k in range(TOP_K):
            row_copy(r, k).wait()
        return c

    lax.fori_loop(0, tm, drain, 0)


def _dispatch(tail_start, dest, hp, xs, zero_tails):
    t, half = hp.shape
    tm = min(ROW_TILE, t)
    grid_spec = pltpu.PrefetchScalarGridSpec(
        num_scalar_prefetch=1,
        grid=(t // tm,),
        in_specs=[pl.BlockSpec((TOP_K, tm), lambda i, tail: (0, i), memory_space=pltpu.SMEM),
                  pl.BlockSpec((tm, half), lambda i, tail: (i, 0)),
                  pl.BlockSpec(memory_space=pl.ANY)],
        out_specs=pl.BlockSpec(memory_space=pl.ANY),
        scratch_shapes=[pltpu.VMEM((EXPERT_BLOCK, half), u32), pltpu.SemaphoreType.DMA(()),
                        pltpu.SemaphoreType.DMA(())],
    )
    return pl.pallas_call(
        functools.partial(_dispatch_kernel, zero_tails=zero_tails),
        grid_spec=grid_spec,
        out_shape=jax.ShapeDtypeStruct(xs.shape, u32),
        input_output_aliases={3: 0},
        compiler_params=_params(1, 32),
        name="dispatch_rows",
    )(tail_start, dest, hp, xs)


def _alloc_kernel(o_ref):
    del o_ref


def _alloc_rows(n_rows, half):
    return pl.pallas_call(
        _alloc_kernel,
        out_specs=pl.BlockSpec(memory_space=pl.ANY),
        out_shape=jax.ShapeDtypeStruct((n_rows, half), u32),
        name="alloc_rows",
    )()


def _expert_kernel(be_ref, nu_ref, xs_ref, wg_ref, wu_ref, wd_ref, o_ref, wgb, wub, wdb):
    i = pl.program_id(0)
    prev = be_ref[jnp.maximum(i - 1, 0)]
    fresh = jnp.logical_or(i == 0, be_ref[i] != prev)

    @pl.when(jnp.logical_and(fresh, i < nu_ref[0]))
    def _():
        wgb[...] = wg_ref[0].astype(bf16)
        wub[...] = wu_ref[0].astype(bf16)
        wdb[...] = wd_ref[0].astype(bf16)

    @pl.when(i < nu_ref[0])
    def _():
        x = _unpack_bf16(xs_ref[...])
        a = (_silu(_dot(x, wgb[...])) * _dot(x, wub[...])).astype(bf16)
        o_ref[...] = _pack_pairs(_dot(a, wdb[...]))

    @pl.when(i >= nu_ref[0])
    def _():
        o_ref[...] = jnp.zeros_like(o_ref)


def _experts(block_expert, n_used, xs, wg, wu, wd):
    n_rows, half = xs.shape
    d = 2 * half
    ff = wg.shape[2]
    blk = EXPERT_BLOCK
    n_blocks = n_rows // blk

    def row_map(i, be, nu):
        return (jnp.minimum(i, nu[0] - 1), 0)

    def w_map(i, be, nu):
        return (be[i], 0, 0)

    grid_spec = pltpu.PrefetchScalarGridSpec(
        num_scalar_prefetch=2,
        grid=(n_blocks,),
        in_specs=[pl.BlockSpec((blk, half), row_map),
                  pl.BlockSpec((1, d, ff), w_map), pl.BlockSpec((1, d, ff), w_map), pl.BlockSpec((1, ff, d), w_map)],
        out_specs=pl.BlockSpec((blk, half), lambda i, be, nu: (i, 0)),
        scratch_shapes=[pltpu.VMEM((d, ff), bf16), pltpu.VMEM((d, ff), bf16), pltpu.VMEM((ff, d), bf16)],
    )
    return pl.pallas_call(
        _expert_kernel,
        grid_spec=grid_spec,
        out_shape=jax.ShapeDtypeStruct((n_rows, half), u32),
        compiler_params=_params(1, 52),
        name="expert_mlp",
    )(block_expert, n_used, xs, wg, wu, wd)


def _combine_kernel(dest_ref, hp_ref, x1_ref, gt_ref, wt_ref, eo_ref, sg_ref, su_ref, sd_ref, y_ref, rows, sem):
    tm, half = hp_ref.shape

    def row_copy(r, k):
        return pltpu.make_async_copy(eo_ref.at[pl.ds(dest_ref[k, r], 1)], rows.at[k, pl.ds(r, 1)], sem)

    def issue(r, c):
        for k in range(TOP_K):
            row_copy(r, k).start()
        return c

    lax.fori_loop(0, tm, issue, 0)

    x = _unpack_bf16(hp_ref[...])
    a = (_silu(_dot(x, sg_ref[...])) * _dot(x, su_ref[...])).astype(bf16)
    y = _dot(a, sd_ref[...])
    y_top, y_bot = y[:, :half], y[:, half:]

    def drain(r, c):
        for k in range(TOP_K):
            row_copy(r, k).wait()
        return c

    lax.fori_loop(0, tm, drain, 0)

    wt = wt_ref[...]
    for k in range(TOP_K):
        top, bot = _unpack_pairs(rows[k])
        y_top = y_top + wt[:, k:k + 1] * top
        y_bot = y_bot + wt[:, k:k + 1] * bot
    gate = gt_ref[0]
    y_ref[:, :half] = x1_ref[:, :half] + gate[:, :half] * y_top
    y_ref[:, half:] = x1_ref[:, half:] + gate[:, half:] * y_bot


def _combine(dest, hp, x1, gate, wt_rows, eo, sg, su, sd, rows_per_mod):
    t, d = x1.shape
    tm = min(COMBINE_TILE, t)
    mod_spec = _mod_spec(gate.shape[1], tm, d, rows_per_mod)
    return pl.pallas_call(
        _combine_kernel,
        grid=(t // tm,),
        in_specs=[pl.BlockSpec((TOP_K, tm), lambda i: (0, i), memory_space=pltpu.SMEM),
                  pl.BlockSpec((tm, d // 2), lambda i: (i, 0)),
                  pl.BlockSpec((tm, d), lambda i: (i, 0)),
                  mod_spec,
                  pl.BlockSpec((tm, TOP_K), lambda i: (i, 0)),
                  pl.BlockSpec(memory_space=pl.ANY),
                  _const_spec(sg.shape), _const_spec(su.shape), _const_spec(sd.shape)],
        out_specs=pl.BlockSpec((tm, d), lambda i: (i, 0)),
        out_shape=jax.ShapeDtypeStruct((t, d), f32),
        scratch_shapes=[pltpu.VMEM((TOP_K, tm, d // 2), u32), pltpu.SemaphoreType.DMA(())],
        compiler_params=_params(1, 48),
        name="combine_shared",
    )(dest, hp, x1, gate, wt_rows, eo, sg, su, sd)


def _moe(groups, counts, wg, wu, wd, sg, su, sd, first_expert):
    n_exp = counts.shape[0]
    blk = EXPERT_BLOCK
    n_tok = sum(g["x1"].shape[0] for g in groups)
    half = groups[0]["hp"].shape[1]
    n_blocks = -(-(n_tok * TOP_K) // blk) + n_exp
    cnt = counts[:, 0].astype(i32)
    padded = (cnt + blk - 1) // blk * blk
    pad_end = jnp.cumsum(padded)
    pad_start = pad_end - padded
    tail_start = jnp.where(padded > 0, pad_end - blk, -1).astype(i32)
    expert_ids = jnp.arange(n_exp, dtype=i32)
    block_rows = jnp.arange(n_blocks, dtype=i32) * blk
    block_expert = jnp.minimum(jnp.sum((pad_end[None, :] <= block_rows[:, None]).astype(i32), axis=1),
                               n_exp - 1).astype(i32) + first_expert
    n_used = (pad_end[-1:] // blk).astype(i32)

    xs = _alloc_rows(n_blocks * blk, half)
    dests = []
    for n, g in enumerate(groups):
        start_of = jnp.sum(jnp.where(g["idx"][None] == expert_ids[:, None, None], pad_start[:, None, None], 0), axis=0)
        dest = start_of + g["rank"]
        dests.append(dest)
        xs = _dispatch(tail_start, dest, g["hp"], xs, zero_tails=(n == 0))
    eo = _experts(block_expert, n_used, xs, wg, wu, wd)
    return [_combine(dest, g["hp"], g["x1"], g["gate"], g["wt"].T, eo, sg, su, sd, g["rows_per_mod"])
            for dest, g in zip(dests, groups)]


def kernel(x_prompt, x_sample, cache_mla_ckv, cache_mla_kpe, cache_fox_k, cache_fox_v, cache_fox_logf, page_table, c_prompt, c_sample, w_ada, b_ada, g_norm_attn, g_norm_moe, w_in, g_q_a, w_q_b, g_q_mla, g_ckv, g_kpe, w_uk, w_uv, g_q_fox, g_k_fox, b_forget, w_o, w_router, b_router, w_exp_gate, w_exp_up, w_exp_down, w_sh_gate, w_sh_up, w_sh_down):
    depth = w_in.shape[0]
    b, s, d = x_prompt.shape
    bd, ds = x_sample.shape[:2]
    assert ds == 1, "the sample group decodes one token per sequence"
    n_pool, page = cache_mla_ckv.shape[1:3]
    n_pages = page_table.shape[1]
    n_fox, hd_fox = cache_fox_k.shape[3:]
    assert hd_fox == HEAD
    fw = n_fox * HEAD
    kr = cache_mla_ckv.shape[3]
    n_mla = w_uk.shape[2]
    hw = n_mla * HEAD
    n_exp = w_router.shape[2]
    assert n_exp % N_GROUPS == 0 and n_exp // N_GROUPS == 8

    cos_p, sin_p = _rope_tables(jnp.arange(s))
    cos_s, sin_s = _rope_tables(jnp.full((1,), n_pages * page))

    xp = x_prompt.reshape(b * s, d)
    xs = x_sample.reshape(bd, d)
    ff = w_exp_gate.shape[3]
    wg_all = w_exp_gate.reshape(depth * n_exp, d, ff)
    wu_all = w_exp_up.reshape(depth * n_exp, d, ff)
    wd_all = w_exp_down.reshape(depth * n_exp, ff, d)
    ckv_all = cache_mla_ckv.reshape(depth * n_pool, page, kr)
    fk_all = cache_fox_k.reshape(depth * n_pool, page * n_fox, HEAD)
    fv_all = cache_fox_v.reshape(depth * n_pool, page * n_fox, HEAD)
    kpe_all = jnp.swapaxes(cache_mla_kpe, 2, 3).reshape(depth * n_pool, ROPE, page)
    lf_all = jnp.swapaxes(cache_fox_logf, 2, 3).reshape(depth * n_pool, n_fox, page)
    st = [[] for _ in range(10)]
    for l in range(depth):
        pw = _pack_proj_weights(w_in[l], g_q_a[l], w_q_b[l], g_q_mla[l], g_ckv[l], g_kpe[l], w_uk[l], w_uv[l],
                                g_q_fox[l], g_k_fox[l], b_forget[l])
        w_o_a = w_o[l][:hw].astype(bf16)
        w_o_b = w_o[l][hw:].astype(bf16)
        w_router_t = w_router[l].T
        b_router_c = b_router[l].reshape(n_exp, 1)
        g_attn = g_norm_attn[l].reshape(1, d)
        g_moe = g_norm_moe[l].reshape(1, d)
        sg, su, sd = w_sh_gate[l].astype(bf16), w_sh_up[l].astype(bf16), w_sh_down[l].astype(bf16)
        moe_w = (wg_all, wu_all, wd_all, sg, su, sd)

        mod = _modulation(jnp.concatenate([c_prompt, c_sample], axis=0), w_ada[l], b_ada[l])
        mod_p = [m.reshape(b, 1, d) for m in jnp.split(mod[:b], 6, axis=-1)]
        mod_s = [m.reshape(1, bd, d) for m in jnp.split(mod[b:], 6, axis=-1)]

        sh1, sc1, gt1, sh2, sc2, gt2 = mod_p
        (ckv, kpe, kf, vf, logf, qm, km, vm, qfb, kfb, vfb) = _project(
            xp, sh1, sc1, g_attn, pw, cos_p, sin_p, rows_per_mod=s, rows_per_pos=s)
        cum = _cumsum_seq(logf.reshape(b, s, n_fox))
        o_mla = _flash(qm, km, vm, b, s, n_mla, MLA_QK, HEAD)
        o_fox = _flash(qfb, kfb, vfb, b, s, n_fox, HEAD, HEAD, cum=cum)
        x1, hp, idx, wt, rank, counts = _oproj_router(o_mla, o_fox, w_o_a, w_o_b, xp, gt1, sh2, sc2, g_moe,
                                                      w_router_t, b_router_c, jnp.zeros((n_exp, 1), f32),
                                                      rows_per_mod=s)
        group_p = dict(hp=hp, idx=idx, wt=wt, rank=rank, x1=x1, gate=gt2, rows_per_mod=s)
        for lst, arr in zip(st[:5], (ckv.reshape(b, s, kr), kpe.reshape(b, s, ROPE), kf.reshape(b, s, n_fox, HEAD),
                                     vf.reshape(b, s, n_fox, HEAD), logf.reshape(b, s, n_fox))):
            lst.append(arr)

        sh1, sc1, gt1, sh2, sc2, gt2 = mod_s
        (ckv, kpe, kf, vf, logf, qm, km, vm, qfb, kfb, vfb) = _project(
            xs, sh1, sc1, g_attn, pw, cos_s, sin_s, rows_per_mod=bd, rows_per_pos=bd)
        q_lat = _absorb(qm, pw["w_uk"], n_mla).reshape(bd, n_mla, kr)
        q_pe = qm.reshape(bd, n_mla, MLA_QK)[:, :, NOPE:]
        o_lat, o_fox = _decode(page_table + l * n_pool, q_lat, q_pe, qfb.reshape(bd, n_fox, HEAD),
                               logf.reshape(bd, n_fox, 1), ckv.reshape(bd, 1, kr), kpe.reshape(bd, 1, ROPE),
                               kf.reshape(bd, n_fox, HEAD), vf.reshape(bd, n_fox, HEAD),
                               ckv_all, kpe_all, fk_all, fv_all, lf_all)
        mix_mla, mix_fox = _expand(o_lat.reshape(bd, n_mla * kr), o_fox.reshape(bd, fw), pw["w_uv"], n_mla)
        x1, hp, idx, wt, rank, counts = _oproj_router(mix_mla, mix_fox, w_o_a, w_o_b, xs, gt1, sh2, sc2, g_moe,
                                                      w_router_t, b_router_c, counts, rows_per_mod=bd)
        group_s = dict(hp=hp, idx=idx, wt=wt, rank=rank, x1=x1, gate=gt2, rows_per_mod=bd)
        xp, xs = _moe([group_p, group_s], counts, *moe_w, first_expert=l * n_exp)
        for lst, arr in zip(st[5:], (ckv.reshape(bd, 1, kr), kpe.reshape(bd, 1, ROPE), kf.reshape(bd, 1, n_fox, HEAD),
                                     vf.reshape(bd, 1, n_fox, HEAD), logf.reshape(bd, 1, n_fox))):
            lst.append(arr)

    stacked = [jnp.stack(a, 0) for a in st]
    return (xp.reshape(b, s, d), xs.reshape(bd, 1, d), *stacked)
```

```python
import functools

import jax
import jax.numpy as jnp
from jax import lax
from jax.experimental import pallas as pl
from jax.experimental.pallas import tpu as pltpu

f32 = jnp.float32
bf16 = jnp.bfloat16
i32 = jnp.int32
u32 = jnp.uint32

LANES = 128
NOPE = 128
ROPE = 64
HEAD = 128
MLA_QK = 2 * LANES
MLA_SCALE = (NOPE + ROPE) ** -0.5
FOX_SCALE = HEAD ** -0.5
ROPE_THETA = 10000.0
N_GROUPS = 8
TOPK_GROUPS = 4
TOP_K = 8
ROUTED_SCALE = 2.5
NORM_EPS = 1e-6
NEG_INF = float("-inf")

ROW_TILE = 256
ATT_TILE = 512
ATT_HEADS = 2
EXPERT_BLOCK = 256
COMBINE_TILE = 128
DECODE_PAGES = 8
MOD_TILE = 1024


def _params(n_axes, vmem_mb):
    return pltpu.CompilerParams(dimension_semantics=("arbitrary",) * n_axes,
                                vmem_limit_bytes=vmem_mb << 20)


def _const_spec(shape):
    nd = len(shape)
    return pl.BlockSpec(shape, lambda *_: (0,) * nd, pipeline_mode=pl.Buffered(1))


def _mod_spec(mod_rows, tm, d, rows_per_mod):
    if mod_rows == 1:
        tiles_per_mod = rows_per_mod // tm
        return pl.BlockSpec((1, 1, d), lambda i: (i // tiles_per_mod, 0, 0))
    return pl.BlockSpec((1, tm, d), lambda i: (0, i, 0))


def _sigmoid(x):
    return 1.0 / (1.0 + jnp.exp(-x))


def _silu(x):
    return x * _sigmoid(x)


def _rms(x, width):
    return lax.rsqrt(jnp.sum(x * x, axis=-1, keepdims=True) * (1.0 / width) + NORM_EPS)


def _dot(a, b):
    return jnp.dot(a, b, preferred_element_type=f32)


def _dot_nt(a, b):
    return lax.dot_general(a, b, (((1,), (1,)), ((), ())), preferred_element_type=f32)


def _floor_div(x, n):
    if n & (n - 1) == 0:
        return jnp.right_shift(x, n.bit_length() - 1)
    return x // n


def _split3(x):
    hi = x.astype(bf16)
    r = x - hi.astype(f32)
    mid = r.astype(bf16)
    lo = (r - mid.astype(f32)).astype(bf16)
    return hi, mid, lo


def _mod_kernel(c_ref, w_ref, b_ref, o_ref):
    a = _silu(c_ref[...]).astype(bf16)
    o_ref[...] = _dot(a, w_ref[...].astype(bf16)) + b_ref[...]


def _modulation(c, w_ada, b_ada):
    m, d = c.shape
    n = w_ada.shape[1]
    tn = min(MOD_TILE, n)
    return pl.pallas_call(
        _mod_kernel,
        grid=(n // tn,),
        in_specs=[pl.BlockSpec((m, d), lambda j: (0, 0)),
                  pl.BlockSpec((d, tn), lambda j: (0, j)),
                  pl.BlockSpec((1, tn), lambda j: (0, j))],
        out_specs=pl.BlockSpec((m, tn), lambda j: (0, j)),
        out_shape=jax.ShapeDtypeStruct((m, n), f32),
        compiler_params=_params(1, 40),
        name="modulation",
    )(c, w_ada, b_ada.reshape(1, n))


def _proj_kernel(x_ref, sh_ref, sc_ref, gn_ref, win_ref, gqa_ref, wqb_ref, gckv_ref, wuk_ref, wuv_ref,
                 small_ref, cos_ref, sin_ref,
                 ckv_ref, kpe_ref, kf_ref, vf_ref, logf_ref,
                 qm_ref, km_ref, vm_ref, qfb_ref, kfb_ref, vfb_ref, *, dims):
    d, qr, kr, n_mla, n_fox = dims
    fw = n_fox * HEAD
    x = x_ref[...]
    h = x * _rms(x, d) * gn_ref[...]
    h = h * (1.0 + sc_ref[0]) + sh_ref[0]
    hb = h.astype(bf16)
    cos = cos_ref[...]
    sin = sin_ref[...]
    g_qn, g_pea, g_peb = small_ref[0:1, :], small_ref[1:2, :], small_ref[2:3, :]
    g_ka, g_kb = small_ref[3:4, :], small_ref[4:5, :]
    g_qf, g_kf, b_f = small_ref[5:6, :], small_ref[6:7, :], small_ref[7:8, :]

    o_ckv = qr
    o_qf = o_ckv + kr
    o_kf = o_qf + fw
    o_vf = o_kf + fw
    o_ka = o_vf + fw
    o_kb = o_ka + LANES
    o_f = o_kb + LANES

    ka = _dot(hb, win_ref[:, o_ka:o_ka + LANES])
    kb = _dot(hb, win_ref[:, o_kb:o_kb + LANES])
    r = _rms(ka, ROPE)
    kpe = (ka * r * g_ka) * cos + (kb * r * g_kb) * sin
    kpe_ref[...] = kpe[:, :ROPE]
    kpe_b = kpe.astype(bf16)

    c = _dot(hb, win_ref[:, o_ckv:o_ckv + kr])
    cn = c * _rms(c, kr) * gckv_ref[...]
    ckv_ref[...] = cn
    cb = cn.astype(bf16)
    k_nope = _dot(cb, wuk_ref[...])
    vm_ref[...] = _dot(cb, wuv_ref[...]).astype(bf16)

    qa = _dot(hb, win_ref[:, 0:qr])
    qa = (qa * _rms(qa, qr) * gqa_ref[...]).astype(bf16)
    hw = n_mla * LANES
    for hd in range(n_mla):
        lo = hd * LANES
        qn = _dot(qa, wqb_ref[:, lo:lo + LANES])
        pa = _dot(qa, wqb_ref[:, hw + lo:hw + lo + LANES])
        pb = _dot(qa, wqb_ref[:, 2 * hw + lo:2 * hw + lo + LANES])
        ss = jnp.sum(qn * qn, axis=-1, keepdims=True) + jnp.sum(pa * pa, axis=-1, keepdims=True)
        r = lax.rsqrt(ss * (1.0 / (NOPE + ROPE)) + NORM_EPS)
        q_nope = qn * r * g_qn
        q_pe = (pa * r * g_pea) * cos + (pb * r * g_peb) * sin
        base = hd * MLA_QK
        qm_ref[:, base:base + LANES] = (q_nope * MLA_SCALE).astype(bf16)
        qm_ref[:, base + LANES:base + MLA_QK] = (q_pe * MLA_SCALE).astype(bf16)
        km_ref[:, base:base + LANES] = k_nope[:, lo:lo + LANES].astype(bf16)
        km_ref[:, base + LANES:base + MLA_QK] = kpe_b

    for hd in range(n_fox):
        lo = hd * HEAD
        qf = _dot(hb, win_ref[:, o_qf + lo:o_qf + lo + HEAD])
        qfb_ref[:, lo:lo + HEAD] = (qf * _rms(qf, HEAD) * g_qf * FOX_SCALE).astype(bf16)
        kf = _dot(hb, win_ref[:, o_kf + lo:o_kf + lo + HEAD])
        kf = kf * _rms(kf, HEAD) * g_kf
        kf_ref[:, lo:lo + HEAD] = kf
        kfb_ref[:, lo:lo + HEAD] = kf.astype(bf16)
    vf = _dot(hb, win_ref[:, o_vf:o_vf + fw])
    vf_ref[...] = vf
    vfb_ref[...] = vf.astype(bf16)
    z = _dot(hb, win_ref[:, o_f:o_f + LANES]) + b_f
    logf = jnp.minimum(z, 0.0) - jnp.log1p(jnp.exp(-jnp.abs(z)))
    logf_ref[...] = logf[:, :n_fox]


def _project(x, shift, scale, g_norm, pw, cos_t, sin_t, rows_per_mod, rows_per_pos):
    t, d = x.shape
    qr, kr, n_mla, n_fox = pw["dims"]
    fw = n_fox * HEAD
    tm = min(ROW_TILE, t)
    tiles_per_pos = max(rows_per_pos // tm, 1)
    pos_rows = cos_t.shape[0]
    tp = min(tm, pos_rows)
    mod_spec = _mod_spec(shift.shape[1], tm, d, rows_per_mod)
    pos_spec = pl.BlockSpec((tp, LANES), lambda i: (i % tiles_per_pos, 0))

    def row_spec(w):
        return pl.BlockSpec((tm, w), lambda i: (i, 0))

    out_widths = [(kr, f32), (ROPE, f32), (fw, f32), (fw, f32), (n_fox, f32),
                  (n_mla * MLA_QK, bf16), (n_mla * MLA_QK, bf16), (n_mla * HEAD, bf16),
                  (fw, bf16), (fw, bf16), (fw, bf16)]
    return pl.pallas_call(
        functools.partial(_proj_kernel, dims=(d, qr, kr, n_mla, n_fox)),
        grid=(t // tm,),
        in_specs=[row_spec(d), mod_spec, mod_spec, _const_spec((1, d)),
                  _const_spec(pw["w_in"].shape), _const_spec((1, qr)), _const_spec(pw["w_qb"].shape),
                  _const_spec((1, kr)), _const_spec(pw["w_uk"].shape), _const_spec(pw["w_uv"].shape),
                  _const_spec((8, LANES)), pos_spec, pos_spec],
        out_specs=[row_spec(w) for w, _ in out_widths],
        out_shape=[jax.ShapeDtypeStruct((t, w), dt) for w, dt in out_widths],
        compiler_params=_params(1, 56),
        name="project",
    )(x, shift, scale, g_norm, pw["w_in"], pw["g_qa"], pw["w_qb"], pw["g_ckv"], pw["w_uk"], pw["w_uv"],
      pw["small"], cos_t, sin_t)


def _pad_lanes(a):
    return jnp.pad(a, [(0, 0)] * (a.ndim - 1) + [(0, LANES - a.shape[-1])])


def _rot_half(a):
    half = a.shape[-1] // 2
    return jnp.concatenate([a[..., half:], a[..., :half]], axis=-1)


def _pack_proj_weights(w_in, g_q_a, w_q_b, g_q_mla, g_ckv, g_kpe, w_uk, w_uv, g_q_fox, g_k_fox, b_forget):
    qr = g_q_a.shape[0]
    kr = g_ckv.shape[0]
    n_mla = w_uk.shape[1]
    n_fox = b_forget.shape[0]
    fw = n_fox * HEAD
    o = 0
    w_qa = w_in[:, o:o + qr]; o += qr
    w_ckv = w_in[:, o:o + kr]; o += kr
    w_kpe = w_in[:, o:o + ROPE]; o += ROPE
    w_fox = w_in[:, o:o + 3 * fw]; o += 3 * fw
    w_f = w_in[:, o:o + n_fox]
    w_in_p = jnp.concatenate([w_qa, w_ckv, w_fox, _pad_lanes(w_kpe), _pad_lanes(_rot_half(w_kpe)),
                              _pad_lanes(w_f)], axis=1).astype(bf16)
    wq = w_q_b.reshape(qr, n_mla, NOPE + ROPE)
    w_nope = wq[:, :, :NOPE].reshape(qr, n_mla * NOPE)
    w_pe = wq[:, :, NOPE:]
    w_pea = _pad_lanes(w_pe).reshape(qr, n_mla * LANES)
    w_peb = _pad_lanes(_rot_half(w_pe)).reshape(qr, n_mla * LANES)
    w_qb_p = jnp.concatenate([w_nope, w_pea, w_peb], axis=1).astype(bf16)
    g_pe = g_q_mla[NOPE:]
    small = jnp.stack([g_q_mla[:NOPE], _pad_lanes(g_pe), _pad_lanes(_rot_half(g_pe)),
                       _pad_lanes(g_kpe), _pad_lanes(_rot_half(g_kpe)),
                       g_q_fox, g_k_fox, _pad_lanes(b_forget)]).astype(f32)
    return {"dims": (qr, kr, n_mla, n_fox), "w_in": w_in_p, "g_qa": g_q_a.reshape(1, qr), "w_qb": w_qb_p,
            "g_ckv": g_ckv.reshape(1, kr), "w_uk": w_uk.reshape(kr, n_mla * NOPE).astype(bf16),
            "w_uv": w_uv.reshape(kr, n_mla * HEAD).astype(bf16), "small": small}


def _rope_tables(pos):
    half = ROPE // 2
    inv = ROPE_THETA ** (-jnp.arange(half, dtype=f32) / half)
    ang = pos.astype(f32)[:, None] * inv[None, :]
    cos, sin = jnp.cos(ang), jnp.sin(ang)
    return _pad_lanes(jnp.concatenate([cos, cos], axis=1)), _pad_lanes(jnp.concatenate([-sin, sin], axis=1))


def _cumsum_kernel(x_ref, o_ref):
    s = x_ref.shape[1]
    x = x_ref[0]
    tri = (lax.broadcasted_iota(i32, (s, s), 1) <= lax.broadcasted_iota(i32, (s, s), 0)).astype(bf16)
    hi, mid, lo = _split3(x)
    o_ref[0] = _dot(tri, hi) + _dot(tri, mid) + _dot(tri, lo)


def _cumsum_seq(x):
    b, s, h = x.shape
    return pl.pallas_call(
        _cumsum_kernel,
        grid=(b,),
        in_specs=[pl.BlockSpec((1, s, h), lambda i: (i, 0, 0))],
        out_specs=pl.BlockSpec((1, s, h), lambda i: (i, 0, 0)),
        out_shape=jax.ShapeDtypeStruct((b, s, h), f32),
        compiler_params=_params(1, 40),
        name="cumsum_logf",
    )(x)


def _flash_kernel(*refs, has_bias, dk, dv):
    if has_bias:
        q_ref, k_ref, v_ref, cq_ref, ck_ref, o_ref = refs
    else:
        q_ref, k_ref, v_ref, o_ref = refs
    tq = q_ref.shape[0]
    n_heads = q_ref.shape[1] // dk
    i = pl.program_id(2)

    def step(j, carry, diagonal):
        start = pl.multiple_of(j * tq, tq)
        if diagonal:
            keep = lax.broadcasted_iota(i32, (tq, tq), 1) <= lax.broadcasted_iota(i32, (tq, tq), 0)
        out = []
        for hd in range(n_heads):
            m, l, acc = carry[hd]
            s = _dot_nt(q_ref[:, hd * dk:(hd + 1) * dk], k_ref[pl.ds(start, tq), hd * dk:(hd + 1) * dk])
            if has_bias:
                s = s + (cq_ref[0, hd] - ck_ref[0, hd, pl.ds(j, 1), :])
            if diagonal:
                s = jnp.where(keep, s, NEG_INF)
            m_new = jnp.maximum(m, jnp.max(s, axis=-1, keepdims=True))
            p = jnp.exp(s - m_new)
            corr = jnp.exp(m - m_new)
            l = l * corr + jnp.sum(p, axis=-1, keepdims=True)
            acc = acc * corr + _dot(p.astype(bf16), v_ref[pl.ds(start, tq), hd * dv:(hd + 1) * dv])
            out.append((m_new, l, acc))
        return tuple(out)

    init = tuple((jnp.full((tq, 1), NEG_INF, f32), jnp.zeros((tq, 1), f32), jnp.zeros((tq, dv), f32))
                 for _ in range(n_heads))
    carry = step(i, init, True)
    carry = lax.fori_loop(0, i, lambda j, c: step(j, c, False), carry)
    for hd in range(n_heads):
        m, l, acc = carry[hd]
        o_ref[:, hd * dv:(hd + 1) * dv] = (acc / l).astype(o_ref.dtype)


def _flash(q, k, v, batch, seq, n_heads, dk, dv, cum=None):
    t = q.shape[0]
    tq = min(ATT_TILE, seq)
    nq = seq // tq
    hb = ATT_HEADS if n_heads % ATT_HEADS == 0 else 1
    in_specs = [pl.BlockSpec((tq, hb * dk), lambda b, h, i: (b * nq + i, h)),
                pl.BlockSpec((seq, hb * dk), lambda b, h, i: (b, h)),
                pl.BlockSpec((seq, hb * dv), lambda b, h, i: (b, h))]
    args = [q, k, v]
    if cum is not None:
        cum_t = cum.transpose(0, 2, 1)
        in_specs += [pl.BlockSpec((1, hb, tq, 1), lambda b, h, i: (b, h, i, 0)),
                     pl.BlockSpec((1, hb, nq, tq), lambda b, h, i: (b, h, 0, 0))]
        args += [cum_t.reshape(batch, n_heads, seq, 1), cum_t.reshape(batch, n_heads, nq, tq)]
    return pl.pallas_call(
        functools.partial(_flash_kernel, has_bias=cum is not None, dk=dk, dv=dv),
        grid=(batch, n_heads // hb, nq),
        in_specs=in_specs,
        out_specs=pl.BlockSpec((tq, hb * dv), lambda b, h, i: (b * nq + i, h)),
        out_shape=jax.ShapeDtypeStruct((t, n_heads * dv), bf16),
        compiler_params=_params(3, 40),
        name="flash_bias" if cum is not None else "flash",
    )(*args)


def _absorb_kernel(q_ref, wuk_ref, o_ref, *, n_mla):
    kr = wuk_ref.shape[0]
    for hd in range(n_mla):
        qn = q_ref[:, hd * MLA_QK:hd * MLA_QK + NOPE]
        o_ref[:, hd * kr:(hd + 1) * kr] = _dot_nt(qn, wuk_ref[:, hd * NOPE:(hd + 1) * NOPE]).astype(bf16)


def _absorb(qm, w_uk2, n_mla):
    bd = qm.shape[0]
    kr = w_uk2.shape[0]
    return pl.pallas_call(
        functools.partial(_absorb_kernel, n_mla=n_mla),
        out_shape=jax.ShapeDtypeStruct((bd, n_mla * kr), bf16),
        name="absorb_query",
    )(qm, w_uk2)


def _expand_kernel(olat_ref, ofox_ref, wuv_ref, mla_ref, fox_ref, *, n_mla):
    kr = wuv_ref.shape[0]
    for hd in range(n_mla):
        o = olat_ref[:, hd * kr:(hd + 1) * kr].astype(bf16)
        mla_ref[:, hd * HEAD:(hd + 1) * HEAD] = _dot(o, wuv_ref[:, hd * HEAD:(hd + 1) * HEAD]).astype(bf16)
    fox_ref[...] = ofox_ref[...].astype(bf16)


def _expand(o_lat, o_fox, w_uv2, n_mla):
    bd = o_lat.shape[0]
    return pl.pallas_call(
        functools.partial(_expand_kernel, n_mla=n_mla),
        out_shape=[jax.ShapeDtypeStruct((bd, n_mla * HEAD), bf16),
                   jax.ShapeDtypeStruct(o_fox.shape, bf16)],
        name="expand_latent",
    )(o_lat, o_fox, w_uv2)


def _decode_kernel(pt_ref, qlat_ref, qpe_ref, qf_ref, fnew_ref, ckvn_ref, kpen_ref, kfn_ref, vfn_ref,
                   ckv_hbm, kpe_hbm, fk_hbm, fv_hbm, lf_hbm, olat_ref, ofox_ref,
                   ckv_buf, kpe_buf, fk_buf, fv_buf, lf_buf, sems, m_m, l_m, acc_m, m_f, l_f, acc_f, suf,
                   *, n_pages_step, n_fox):
    g = n_pages_step
    b = pl.program_id(0)
    j = pl.program_id(1)
    steps = pl.num_programs(1)
    n_pages = steps * g
    step = b * steps + j
    slot = lax.rem(step, 2)
    page = ckv_buf.shape[2]
    rows = page * n_fox

    def page_copies(bb, jj, sl):
        copies = []
        for p in range(g):
            pid = pt_ref[bb, n_pages - 1 - (jj * g + p)]
            for hbm, buf in ((ckv_hbm, ckv_buf), (kpe_hbm, kpe_buf), (fk_hbm, fk_buf), (fv_hbm, fv_buf),
                             (lf_hbm, lf_buf)):
                copies.append(pltpu.make_async_copy(hbm.at[pid], buf.at[sl, p], sems.at[sl]))
        return copies

    @pl.when(step == 0)
    def _():
        for c in page_copies(0, 0, 0):
            c.start()

    @pl.when(step + 1 < pl.num_programs(0) * steps)
    def _():
        wrap = j + 1 == steps
        for c in page_copies(jnp.where(wrap, b + 1, b), jnp.where(wrap, 0, j + 1), 1 - slot):
            c.start()

    for c in page_copies(b, j, slot):
        c.wait()
    ckv_refs = [ckv_buf.at[slot, p] for p in range(g)]
    kpe_refs = [kpe_buf.at[slot, p] for p in range(g)]
    fk_refs = [fk_buf.at[slot, p] for p in range(g)]
    fv_refs = [fv_buf.at[slot, p] for p in range(g)]
    lf_refs = [lf_buf.at[slot, p] for p in range(g)]

    q_lat = qlat_ref[0]
    q_pe = qpe_ref[0][:, :ROPE]
    q_fox = qf_ref[0]
    f_new = fnew_ref[0]

    @pl.when(j == 0)
    def _():
        ckv_n = ckvn_ref[0].astype(bf16).astype(f32)
        kpe_n = kpen_ref[0].astype(bf16).astype(f32)
        s_m = (jnp.sum(q_lat.astype(f32) * ckv_n, axis=-1, keepdims=True)
               + jnp.sum(q_pe.astype(f32) * kpe_n, axis=-1, keepdims=True))
        m_m[...] = s_m
        l_m[...] = jnp.ones_like(s_m)
        acc_m[...] = jnp.broadcast_to(ckv_n, acc_m.shape)
        kf_n = kfn_ref[0].astype(bf16).astype(f32)
        s_f = jnp.sum(q_fox.astype(f32) * kf_n, axis=-1, keepdims=True)
        m_f[...] = s_f
        l_f[...] = jnp.ones_like(s_f)
        acc_f[...] = vfn_ref[0].astype(bf16).astype(f32)
        suf[...] = jnp.zeros_like(suf)

    col = lax.broadcasted_iota(i32, (n_fox, rows), 1)
    own = (col - _floor_div(col, n_fox) * n_fox) == lax.broadcasted_iota(i32, (n_fox, rows), 0)
    later = (lax.broadcasted_iota(i32, (page, rows), 0)
             > _floor_div(lax.broadcasted_iota(i32, (page, rows), 1), n_fox)).astype(bf16)
    suffix = suf[...]
    s_mla, s_fox, ckv_b, fv_b = [], [], [], []
    for p in range(g):
        cb = ckv_refs[p][...].astype(bf16)
        ckv_b.append(cb)
        s_mla.append(_dot_nt(q_lat, cb) + _dot(q_pe, kpe_refs[p][...].astype(bf16)))
        lf = lf_refs[p][...]
        hi, mid, lo = _split3(lf)
        parts = jnp.concatenate([hi.astype(f32), mid.astype(f32), lo.astype(f32)], axis=0).astype(bf16)
        sums = _dot(parts, later)
        after = sums[0:n_fox] + sums[n_fox:2 * n_fox] + sums[2 * n_fox:3 * n_fox]
        s = _dot_nt(q_fox, fk_refs[p][...].astype(bf16)) + (after + (suffix + f_new))
        s_fox.append(jnp.where(own, s, NEG_INF))
        suffix = suffix + jnp.sum(lf, axis=-1, keepdims=True)
        fv_b.append(fv_refs[p][...].astype(bf16))
    suf[...] = suffix

    def update(s_list, vals, m_ref, l_ref, acc_ref):
        width = s_list[0].shape[1]
        s = jnp.concatenate(s_list, axis=1)
        m_old = m_ref[...]
        m_new = jnp.maximum(m_old, jnp.max(s, axis=-1, keepdims=True))
        corr = jnp.exp(m_old - m_new)
        pr = jnp.exp(s - m_new)
        l_ref[...] = l_ref[...] * corr + jnp.sum(pr, axis=-1, keepdims=True)
        pb = pr.astype(bf16)
        acc = acc_ref[...] * corr
        for p in range(g):
            acc = acc + _dot(pb[:, p * width:(p + 1) * width], vals[p])
        acc_ref[...] = acc
        m_ref[...] = m_new

    update(s_mla, ckv_b, m_m, l_m, acc_m)
    update(s_fox, fv_b, m_f, l_f, acc_f)

    @pl.when(j == pl.num_programs(1) - 1)
    def _():
        olat_ref[0] = acc_m[...] / l_m[...]
        ofox_ref[0] = acc_f[...] / l_f[...]


def _decode(page_table, q_lat, q_pe, q_fox, f_new, ckv_new, kpe_new, kf_new, vf_new,
            cache_ckv, cache_kpe, cache_fk, cache_fv, cache_logf):
    bd, n_pages = page_table.shape
    n_mla, kr = q_lat.shape[1:]
    n_fox = q_fox.shape[1]
    page = cache_ckv.shape[1]
    g = min(DECODE_PAGES, n_pages)
    steps = n_pages // g

    def per_batch(shape):
        nd = len(shape)
        return pl.BlockSpec((1,) + tuple(shape[1:]), lambda b, j, pt: (b,) + (0,) * (nd - 1))

    small_in = [q_lat, q_pe, q_fox, f_new, ckv_new, kpe_new, kf_new, vf_new]
    caches = [cache_ckv, cache_kpe, cache_fk, cache_fv, cache_logf]
    in_specs = [per_batch(a.shape) for a in small_in] + [pl.BlockSpec(memory_space=pl.ANY)] * len(caches)
    page_bufs = [pltpu.VMEM((2, g) + tuple(arr.shape[1:]), f32) for arr in caches]
    grid_spec = pltpu.PrefetchScalarGridSpec(
        num_scalar_prefetch=1,
        grid=(bd, steps),
        in_specs=in_specs,
        out_specs=[pl.BlockSpec((1, n_mla, kr), lambda b, j, pt: (b, 0, 0)),
                   pl.BlockSpec((1, n_fox, HEAD), lambda b, j, pt: (b, 0, 0))],
        scratch_shapes=page_bufs + [
            pltpu.SemaphoreType.DMA((2,)),
            pltpu.VMEM((n_mla, 1), f32), pltpu.VMEM((n_mla, 1), f32), pltpu.VMEM((n_mla, kr), f32),
            pltpu.VMEM((n_fox, 1), f32), pltpu.VMEM((n_fox, 1), f32), pltpu.VMEM((n_fox, HEAD), f32),
            pltpu.VMEM((n_fox, 1), f32)],
    )
    return pl.pallas_call(
        functools.partial(_decode_kernel, n_pages_step=g, n_fox=n_fox),
        grid_spec=grid_spec,
        out_shape=[jax.ShapeDtypeStruct((bd, n_mla, kr), f32), jax.ShapeDtypeStruct((bd, n_fox, HEAD), f32)],
        compiler_params=_params(2, 48),
        name="paged_decode",
    )(page_table, *small_in, *caches)


def _pack_pairs(h):
    half = h.shape[1] // 2
    hb = h.astype(bf16).astype(f32)
    top = pltpu.bitcast(hb[:, :half], u32)
    bot = pltpu.bitcast(hb[:, half:], u32)
    return top | (bot >> 16)


def _unpack_pairs(w):
    top = pltpu.bitcast(w & jnp.uint32(0xFFFF0000), f32)
    bot = pltpu.bitcast(w << 16, f32)
    return top, bot


def _unpack_bf16(w):
    top, bot = _unpack_pairs(w)
    return jnp.concatenate([top, bot], axis=1).astype(bf16)


def _oproj_kernel(mla_ref, fox_ref, woa_ref, wob_ref, x_ref, gt_ref, sh_ref, sc_ref, gn_ref, wr_ref, br_ref,
                  cin_ref, x1_ref, hp_ref, idx_ref, wt_ref, rank_ref, cnt_ref, carry_ref):
    d = x_ref.shape[1]
    tm = x_ref.shape[0]
    n_exp = wr_ref.shape[0]
    per_group = n_exp // N_GROUPS

    @pl.when(pl.program_id(0) == 0)
    def _():
        carry_ref[...] = cin_ref[...]

    mix = _dot(mla_ref[...], woa_ref[...]) + _dot(fox_ref[...], wob_ref[...])
    x1 = x_ref[...] + gt_ref[0] * mix
    x1_ref[...] = x1
    h = x1 * _rms(x1, d) * gn_ref[...]
    h = h * (1.0 + sc_ref[0]) + sh_ref[0]
    hp_ref[...] = _pack_pairs(h)

    h_hi = h.astype(bf16)
    h_lo = (h - h_hi.astype(f32)).astype(bf16)
    wr = wr_ref[...]
    w_hi = wr.astype(bf16)
    w_lo = (wr - w_hi.astype(f32)).astype(bf16)
    logits = _dot_nt(w_hi, h_hi) + (_dot_nt(w_hi, h_lo) + _dot_nt(w_lo, h_hi))
    scores = _sigmoid(logits).reshape(N_GROUPS, per_group, tm)
    sel = scores + br_ref[...].reshape(N_GROUPS, per_group, 1)

    in_group = lax.broadcasted_iota(i32, (N_GROUPS, per_group, tm), 1).astype(f32)
    group_id = lax.broadcasted_iota(i32, (N_GROUPS, per_group, tm), 0).astype(f32)
    expert_id = group_id * per_group + in_group

    m1 = jnp.max(sel, axis=1, keepdims=True)
    first = jnp.min(jnp.where(sel == m1, in_group, float(per_group)), axis=1, keepdims=True)
    m2 = jnp.max(jnp.where(in_group == first, NEG_INF, sel), axis=1, keepdims=True)
    gscore = m1 + m2
    gid = lax.broadcasted_iota(i32, (N_GROUPS, 1, tm), 0).astype(f32)
    gkeep = jnp.zeros((N_GROUPS, 1, tm), f32)
    for _ in range(TOPK_GROUPS):
        best = jnp.max(gscore, axis=0, keepdims=True)
        pick = jnp.min(jnp.where(gscore == best, gid, float(N_GROUPS)), axis=0, keepdims=True)
        hit = gid == pick
        gkeep = jnp.where(hit, 1.0, gkeep)
        gscore = jnp.where(hit, NEG_INF, gscore)

    cand = jnp.where(jnp.broadcast_to(gkeep, sel.shape) > 0.5, sel, NEG_INF)
    chosen = jnp.zeros((N_GROUPS, per_group, tm), f32)
    picks, weights = [], []
    for _ in range(TOP_K):
        best = jnp.max(jnp.max(cand, axis=1, keepdims=True), axis=0, keepdims=True)
        pick = jnp.min(jnp.min(jnp.where(cand == best, expert_id, float(n_exp)), axis=1, keepdims=True),
                       axis=0, keepdims=True)
        hit = expert_id == pick
        weights.append(jnp.sum(jnp.sum(jnp.where(hit, scores, 0.0), axis=1, keepdims=True), axis=0))
        picks.append(pick)
        chosen = jnp.where(hit, 1.0, chosen)
        cand = jnp.where(hit, NEG_INF, cand)

    wsum = weights[0]
    for w in weights[1:]:
        wsum = wsum + w
    chosen2 = chosen.reshape(n_exp, tm)
    upto = (lax.broadcasted_iota(i32, (tm, tm), 0) <= lax.broadcasted_iota(i32, (tm, tm), 1)).astype(bf16)
    incl = _dot(chosen2.astype(bf16), upto)
    before = (carry_ref[...] + (incl - chosen2)).reshape(N_GROUPS, per_group, tm)
    carry_ref[...] = carry_ref[...] + jnp.sum(chosen2, axis=1, keepdims=True)
    cnt_ref[...] = carry_ref[...]
    for k in range(TOP_K):
        hit = expert_id == picks[k]
        rk = jnp.sum(jnp.sum(jnp.where(hit, before, 0.0), axis=1, keepdims=True), axis=0)
        idx_ref[k:k + 1, :] = picks[k][0].astype(i32)
        wt_ref[k:k + 1, :] = weights[k] / wsum * ROUTED_SCALE
        rank_ref[k:k + 1, :] = rk.astype(i32)


def _oproj_router(mla, fox, w_o_a, w_o_b, x, gate, shift, scale, g_norm, w_router_t, b_router, counts_in,
                  rows_per_mod):
    t, d = x.shape
    tm = min(ROW_TILE, t)
    n_exp = w_router_t.shape[0]
    mod_spec = _mod_spec(shift.shape[1], tm, d, rows_per_mod)

    def row_spec(w):
        return pl.BlockSpec((tm, w), lambda i: (i, 0))

    def tok_spec():
        return pl.BlockSpec((TOP_K, tm), lambda i: (0, i))

    return pl.pallas_call(
        _oproj_kernel,
        grid=(t // tm,),
        in_specs=[row_spec(mla.shape[1]), row_spec(fox.shape[1]), _const_spec(w_o_a.shape), _const_spec(w_o_b.shape),
                  row_spec(d), mod_spec, mod_spec, mod_spec, _const_spec((1, d)),
                  _const_spec((n_exp, d)), _const_spec((n_exp, 1)), _const_spec((n_exp, 1))],
        out_specs=[row_spec(d), row_spec(d // 2), tok_spec(), tok_spec(), tok_spec(),
                   pl.BlockSpec((n_exp, 1), lambda i: (0, 0))],
        out_shape=[jax.ShapeDtypeStruct((t, d), f32), jax.ShapeDtypeStruct((t, d // 2), u32),
                   jax.ShapeDtypeStruct((TOP_K, t), i32), jax.ShapeDtypeStruct((TOP_K, t), f32),
                   jax.ShapeDtypeStruct((TOP_K, t), i32), jax.ShapeDtypeStruct((n_exp, 1), f32)],
        scratch_shapes=[pltpu.VMEM((n_exp, 1), f32)],
        compiler_params=_params(1, 48),
        name="oproj_router",
    )(mla, fox, w_o_a, w_o_b, x, gate, shift, scale, g_norm, w_router_t, b_router, counts_in)


def _dispatch_kernel(zf_ref, dest_ref, hp_ref, *rest, creates):
    if creates:
        xs_ref, zeros_ref, sem, zsem = rest
    else:
        _, xs_ref, zeros_ref, sem, zsem = rest
    tm = hp_ref.shape[0]
    blk = zeros_ref.shape[0]

    if creates:
        @pl.when(pl.program_id(0) == 0)
        def _():
            zeros_ref[...] = jnp.zeros_like(zeros_ref)

            def zero_copy(i):
                return pltpu.make_async_copy(zeros_ref, xs_ref.at[pl.ds(pl.multiple_of(i * blk, blk), blk)], zsem)

            def issue_zero(i, c):
                @pl.when(zf_ref[i] > 0)
                def _():
                    zero_copy(i).start()
                return c

            def drain_zero(i, c):
                @pl.when(zf_ref[i] > 0)
                def _():
                    zero_copy(i).wait()
                return c

            lax.fori_loop(0, zf_ref.shape[0], issue_zero, 0)
            lax.fori_loop(0, zf_ref.shape[0], drain_zero, 0)

    def row_copy(r, k):
        return pltpu.make_async_copy(hp_ref.at[pl.ds(r, 1)], xs_ref.at[pl.ds(dest_ref[k, r], 1)], sem)

    def issue(r, c):
        for k in range(TOP_K):
            row_copy(r, k).start()
        return c

    lax.fori_loop(0, tm, issue, 0)

    def drain(r, c):
        for k in range(TOP_K):
            row_copy(r, k).wait()
        return c

    lax.fori_loop(0, tm, drain, 0)


def _dispatch(zero_flag, dest, hp, xs=None):
    t, half = hp.shape
    tm = min(ROW_TILE, t)
    creates = xs is None
    n_rows = zero_flag.shape[0] * EXPERT_BLOCK
    in_specs = [pl.BlockSpec((TOP_K, tm), lambda i, zf: (0, i), memory_space=pltpu.SMEM),
                pl.BlockSpec((tm, half), lambda i, zf: (i, 0))]
    if not creates:
        in_specs.append(pl.BlockSpec(memory_space=pl.ANY))
    grid_spec = pltpu.PrefetchScalarGridSpec(
        num_scalar_prefetch=1,
        grid=(t // tm,),
        in_specs=in_specs,
        out_specs=pl.BlockSpec(memory_space=pl.ANY),
        scratch_shapes=[pltpu.VMEM((EXPERT_BLOCK, half), u32), pltpu.SemaphoreType.DMA(()),
                        pltpu.SemaphoreType.DMA(())],
    )
    return pl.pallas_call(
        functools.partial(_dispatch_kernel, creates=creates),
        grid_spec=grid_spec,
        out_shape=jax.ShapeDtypeStruct((n_rows, half), u32),
        input_output_aliases={} if creates else {3: 0},
        compiler_params=_params(1, 32),
        name="dispatch_rows",
    )(zero_flag, dest, hp, *(() if creates else (xs,)))


def _expert_kernel(be_ref, nx_ref, gp_ref, nu_ref, xs_ref, wg_hbm, wu_hbm, wd_hbm, o_ref,
                   wgf, wuf, wdf, wgb, wub, wdb, sems):
    i = pl.program_id(0)

    def weight_copies(e, sl):
        return [pltpu.make_async_copy(hbm.at[e], buf.at[sl], sems.at[sl])
                for hbm, buf in ((wg_hbm, wgf), (wu_hbm, wuf), (wd_hbm, wdf))]

    @pl.when(i == 0)
    def _():
        for c in weight_copies(be_ref[0], 0):
            c.start()

    @pl.when(jnp.logical_and(nx_ref[i] > -2, i < nu_ref[0]))
    def _():
        sl = gp_ref[i]
        for c in weight_copies(be_ref[i], sl):
            c.wait()

        @pl.when(nx_ref[i] >= 0)
        def _():
            for c in weight_copies(nx_ref[i], 1 - sl):
                c.start()

        wgb[...] = wgf[sl].astype(bf16)
        wub[...] = wuf[sl].astype(bf16)
        wdb[...] = wdf[sl].astype(bf16)

    @pl.when(i < nu_ref[0])
    def _():
        x = _unpack_bf16(xs_ref[...])
        a = (_silu(_dot(x, wgb[...])) * _dot(x, wub[...])).astype(bf16)
        o_ref[...] = _pack_pairs(_dot(a, wdb[...]))

    @pl.when(i >= nu_ref[0])
    def _():
        o_ref[...] = jnp.zeros_like(o_ref)


def _experts(block_expert, next_expert, group_parity, n_used, xs, wg, wu, wd):
    n_rows, half = xs.shape
    d = 2 * half
    ff = wg.shape[2]
    blk = EXPERT_BLOCK
    n_blocks = n_rows // blk

    def row_map(i, be, nx, gp, nu):
        return (jnp.minimum(i, nu[0] - 1), 0)

    grid_spec = pltpu.PrefetchScalarGridSpec(
        num_scalar_prefetch=4,
        grid=(n_blocks,),
        in_specs=[pl.BlockSpec((blk, half), row_map)] + [pl.BlockSpec(memory_space=pl.ANY)] * 3,
        out_specs=pl.BlockSpec((blk, half), lambda i, be, nx, gp, nu: (i, 0)),
        scratch_shapes=[pltpu.VMEM((2, d, ff), f32), pltpu.VMEM((2, d, ff), f32), pltpu.VMEM((2, ff, d), f32),
                        pltpu.VMEM((d, ff), bf16), pltpu.VMEM((d, ff), bf16), pltpu.VMEM((ff, d), bf16),
                        pltpu.SemaphoreType.DMA((2,))],
    )
    return pl.pallas_call(
        _expert_kernel,
        grid_spec=grid_spec,
        out_shape=jax.ShapeDtypeStruct((n_rows, half), u32),
        compiler_params=_params(1, 52),
        name="expert_mlp",
    )(block_expert, next_expert, group_parity, n_used, xs, wg, wu, wd)


def _combine_kernel(dest_ref, hp_ref, x1_ref, gt_ref, wt_ref, eo_ref, sg_ref, su_ref, sd_ref, y_ref, rows, sem):
    tm, half = hp_ref.shape

    def row_copy(r, k):
        return pltpu.make_async_copy(eo_ref.at[pl.ds(dest_ref[k, r], 1)], rows.at[k, pl.ds(r, 1)], sem)

    def issue(r, c):
        for k in range(TOP_K):
            row_copy(r, k).start()
        return c

    lax.fori_loop(0, tm, issue, 0)

    x = _unpack_bf16(hp_ref[...])
    a = (_silu(_dot(x, sg_ref[...])) * _dot(x, su_ref[...])).astype(bf16)
    y = _dot(a, sd_ref[...])
    y_top, y_bot = y[:, :half], y[:, half:]

    def drain(r, c):
        for k in range(TOP_K):
            row_copy(r, k).wait()
        return c

    lax.fori_loop(0, tm, drain, 0)

    wt = wt_ref[...]
    for k in range(TOP_K):
        top, bot = _unpack_pairs(rows[k])
        y_top = y_top + wt[:, k:k + 1] * top
        y_bot = y_bot + wt[:, k:k + 1] * bot
    gate = gt_ref[0]
    y_ref[:, :half] = x1_ref[:, :half] + gate[:, :half] * y_top
    y_ref[:, half:] = x1_ref[:, half:] + gate[:, half:] * y_bot


def _combine(dest, hp, x1, gate, wt_rows, eo, sg, su, sd, rows_per_mod):
    t, d = x1.shape
    tm = min(COMBINE_TILE, t)
    mod_spec = _mod_spec(gate.shape[1], tm, d, rows_per_mod)
    return pl.pallas_call(
        _combine_kernel,
        grid=(t // tm,),
        in_specs=[pl.BlockSpec((TOP_K, tm), lambda i: (0, i), memory_space=pltpu.SMEM),
                  pl.BlockSpec((tm, d // 2), lambda i: (i, 0)),
                  pl.BlockSpec((tm, d), lambda i: (i, 0)),
                  mod_spec,
                  pl.BlockSpec((tm, TOP_K), lambda i: (i, 0)),
                  pl.BlockSpec(memory_space=pl.ANY),
                  _const_spec(sg.shape), _const_spec(su.shape), _const_spec(sd.shape)],
        out_specs=pl.BlockSpec((tm, d), lambda i: (i, 0)),
        out_shape=jax.ShapeDtypeStruct((t, d), f32),
        scratch_shapes=[pltpu.VMEM((TOP_K, tm, d // 2), u32), pltpu.SemaphoreType.DMA(())],
        compiler_params=_params(1, 48),
        name="combine_shared",
    )(dest, hp, x1, gate, wt_rows, eo, sg, su, sd)


def _moe(groups, counts, wg, wu, wd, sg, su, sd, first_expert):
    n_exp = counts.shape[0]
    blk = EXPERT_BLOCK
    n_tok = sum(g["x1"].shape[0] for g in groups)
    n_blocks = -(-(n_tok * TOP_K) // blk) + n_exp
    cnt = counts[:, 0].astype(i32)
    padded = (cnt + blk - 1) // blk * blk
    pad_end = jnp.cumsum(padded)
    pad_start = pad_end - padded
    expert_ids = jnp.arange(n_exp, dtype=i32)
    block_rows = jnp.arange(n_blocks, dtype=i32) * blk
    local_expert = jnp.minimum(jnp.sum((pad_end[None, :] <= block_rows[:, None]).astype(i32), axis=1), n_exp - 1)
    block_expert = (local_expert + first_expert).astype(i32)
    n_used = (pad_end[-1:] // blk).astype(i32)
    nonempty = padded > 0
    parity_e = (jnp.cumsum(nonempty.astype(i32)) - 1) % 2
    later_ne = jnp.logical_and(nonempty[None, :], expert_ids[None, :] > expert_ids[:, None])
    next_e = jnp.min(jnp.where(later_ne, expert_ids[None, :], n_exp), axis=1)
    next_e = jnp.where(next_e < n_exp, next_e + first_expert, -1)
    onehot = (local_expert[:, None] == expert_ids[None, :]).astype(i32)
    is_first = jnp.sum(onehot * pad_start[None, :], axis=1) == block_rows
    is_first = jnp.logical_and(is_first, block_rows < pad_end[-1])
    next_expert = jnp.where(is_first, jnp.sum(onehot * next_e[None, :], axis=1), -2).astype(i32)
    group_parity = jnp.sum(onehot * parity_e[None, :], axis=1).astype(i32)

    is_tail = jnp.sum(onehot * (pad_end - blk)[None, :], axis=1) == block_rows
    zero_flag = jnp.logical_or(is_tail, block_rows >= pad_end[-1]).astype(i32)

    xs = None
    dests = []
    for g in groups:
        start_of = jnp.sum(jnp.where(g["idx"][None] == expert_ids[:, None, None], pad_start[:, None, None], 0), axis=0)
        dest = start_of + g["rank"]
        dests.append(dest)
        xs = _dispatch(zero_flag, dest, g["hp"], xs)
    eo = _experts(block_expert, next_expert, group_parity, n_used, xs, wg, wu, wd)
    return [_combine(dest, g["hp"], g["x1"], g["gate"], g["wt"].T, eo, sg, su, sd, g["rows_per_mod"])
            for dest, g in zip(dests, groups)]


def kernel(x_prompt, x_sample, cache_mla_ckv, cache_mla_kpe, cache_fox_k, cache_fox_v, cache_fox_logf, page_table, c_prompt, c_sample, w_ada, b_ada, g_norm_attn, g_norm_moe, w_in, g_q_a, w_q_b, g_q_mla, g_ckv, g_kpe, w_uk, w_uv, g_q_fox, g_k_fox, b_forget, w_o, w_router, b_router, w_exp_gate, w_exp_up, w_exp_down, w_sh_gate, w_sh_up, w_sh_down):
    depth = w_in.shape[0]
    b, s, d = x_prompt.shape
    bd, ds = x_sample.shape[:2]
    assert ds == 1, "the sample group decodes one token per sequence"
    n_pool, page = cache_mla_ckv.shape[1:3]
    n_pages = page_table.shape[1]
    n_fox, hd_fox = cache_fox_k.shape[3:]
    assert hd_fox == HEAD
    fw = n_fox * HEAD
    kr = cache_mla_ckv.shape[3]
    n_mla = w_uk.shape[2]
    hw = n_mla * HEAD
    n_exp = w_router.shape[2]
    assert n_exp % N_GROUPS == 0 and n_exp // N_GROUPS == 8

    cos_p, sin_p = _rope_tables(jnp.arange(s))
    cos_s, sin_s = _rope_tables(jnp.full((1,), n_pages * page))

    xp = x_prompt.reshape(b * s, d)
    xs = x_sample.reshape(bd, d)
    ff = w_exp_gate.shape[3]
    wg_all = w_exp_gate.reshape(depth * n_exp, d, ff)
    wu_all = w_exp_up.reshape(depth * n_exp, d, ff)
    wd_all = w_exp_down.reshape(depth * n_exp, ff, d)
    ckv_all = cache_mla_ckv.reshape(depth * n_pool, page, kr)
    fk_all = cache_fox_k.reshape(depth * n_pool, page * n_fox, HEAD)
    fv_all = cache_fox_v.reshape(depth * n_pool, page * n_fox, HEAD)
    kpe_all = jnp.swapaxes(cache_mla_kpe, 2, 3).reshape(depth * n_pool, ROPE, page)
    lf_all = jnp.swapaxes(cache_fox_logf, 2, 3).reshape(depth * n_pool, n_fox, page)
    st = [[] for _ in range(10)]
    for l in range(depth):
        pw = _pack_proj_weights(w_in[l], g_q_a[l], w_q_b[l], g_q_mla[l], g_ckv[l], g_kpe[l], w_uk[l], w_uv[l],
                                g_q_fox[l], g_k_fox[l], b_forget[l])
        w_o_a = w_o[l][:hw].astype(bf16)
        w_o_b = w_o[l][hw:].astype(bf16)
        w_router_t = w_router[l].T
        b_router_c = b_router[l].reshape(n_exp, 1)
        g_attn = g_norm_attn[l].reshape(1, d)
        g_moe = g_norm_moe[l].reshape(1, d)
        sg, su, sd = w_sh_gate[l].astype(bf16), w_sh_up[l].astype(bf16), w_sh_down[l].astype(bf16)
        moe_w = (wg_all, wu_all, wd_all, sg, su, sd)

        mod = _modulation(jnp.concatenate([c_prompt, c_sample], axis=0), w_ada[l], b_ada[l])
        mod_p = [m.reshape(b, 1, d) for m in jnp.split(mod[:b], 6, axis=-1)]
        mod_s = [m.reshape(1, bd, d) for m in jnp.split(mod[b:], 6, axis=-1)]

        sh1, sc1, gt1, sh2, sc2, gt2 = mod_p
        (ckv, kpe, kf, vf, logf, qm, km, vm, qfb, kfb, vfb) = _project(
            xp, sh1, sc1, g_attn, pw, cos_p, sin_p, rows_per_mod=s, rows_per_pos=s)
        cum = _cumsum_seq(logf.reshape(b, s, n_fox))
        o_mla = _flash(qm, km, vm, b, s, n_mla, MLA_QK, HEAD)
        o_fox = _flash(qfb, kfb, vfb, b, s, n_fox, HEAD, HEAD, cum=cum)
        x1, hp, idx, wt, rank, counts = _oproj_router(o_mla, o_fox, w_o_a, w_o_b, xp, gt1, sh2, sc2, g_moe,
                                                      w_router_t, b_router_c, jnp.zeros((n_exp, 1), f32),
                                                      rows_per_mod=s)
        group_p = dict(hp=hp, idx=idx, wt=wt, rank=rank, x1=x1, gate=gt2, rows_per_mod=s)
        for lst, arr in zip(st[:5], (ckv.reshape(b, s, kr), kpe.reshape(b, s, ROPE), kf.reshape(b, s, n_fox, HEAD),
                                     vf.reshape(b, s, n_fox, HEAD), logf.reshape(b, s, n_fox))):
            lst.append(arr)

        sh1, sc1, gt1, sh2, sc2, gt2 = mod_s
        (ckv, kpe, kf, vf, logf, qm, km, vm, qfb, kfb, vfb) = _project(
            xs, sh1, sc1, g_attn, pw, cos_s, sin_s, rows_per_mod=bd, rows_per_pos=bd)
        q_lat = _absorb(qm, pw["w_uk"], n_mla).reshape(bd, n_mla, kr)
        q_pe = qm.reshape(bd, n_mla, MLA_QK)[:, :, NOPE:]
        o_lat, o_fox = _decode(page_table + l * n_pool, q_lat, q_pe, qfb.reshape(bd, n_fox, HEAD),
                               logf.reshape(bd, n_fox, 1), ckv.reshape(bd, 1, kr), kpe.reshape(bd, 1, ROPE),
                               kf.reshape(bd, n_fox, HEAD), vf.reshape(bd, n_fox, HEAD),
                               ckv_all, kpe_all, fk_all, fv_all, lf_all)
        mix_mla, mix_fox = _expand(o_lat.reshape(bd, n_mla * kr), o_fox.reshape(bd, fw), pw["w_uv"], n_mla)
        x1, hp, idx, wt, rank, counts = _oproj_router(mix_mla, mix_fox, w_o_a, w_o_b, xs, gt1, sh2, sc2, g_moe,
                                                      w_router_t, b_router_c, counts, rows_per_mod=bd)
        group_s = dict(hp=hp, idx=idx, wt=wt, rank=rank, x1=x1, gate=gt2, rows_per_mod=bd)
        xp, xs = _moe([group_p, group_s], counts, *moe_w, first_expert=l * n_exp)
        for lst, arr in zip(st[5:], (ckv.reshape(bd, 1, kr), kpe.reshape(bd, 1, ROPE), kf.reshape(bd, 1, n_fox, HEAD),
                                     vf.reshape(bd, 1, n_fox, HEAD), logf.reshape(bd, 1, n_fox))):
            lst.append(arr)

    stacked = [jnp.stack(a, 0) for a in st]
    return (xp.reshape(b, s, d), xs.reshape(bd, 1, d), *stacked)
```

```python
import functools

import jax
import jax.numpy as jnp
from jax import lax
from jax.experimental import pallas as pl
from jax.experimental.pallas import tpu as pltpu

f32 = jnp.float32
bf16 = jnp.bfloat16
i32 = jnp.int32
u32 = jnp.uint32

LANES = 128
NOPE = 128
ROPE = 64
HEAD = 128
MLA_QK = 2 * LANES
MLA_SCALE = (NOPE + ROPE) ** -0.5
FOX_SCALE = HEAD ** -0.5
ROPE_THETA = 10000.0
N_GROUPS = 8
TOPK_GROUPS = 4
TOP_K = 8
ROUTED_SCALE = 2.5
NORM_EPS = 1e-6
NEG_INF = float("-inf")

ROW_TILE = 256
ATT_TILE = 512
ATT_HEADS = 2
EXPERT_BLOCK = 256
COMBINE_TILE = 128
DECODE_PAGES = 8
MOD_TILE = 1024


def _params(n_axes, vmem_mb):
    return pltpu.CompilerParams(dimension_semantics=("arbitrary",) * n_axes,
                                vmem_limit_bytes=vmem_mb << 20)


def _const_spec(shape):
    nd = len(shape)
    return pl.BlockSpec(shape, lambda *_: (0,) * nd, pipeline_mode=pl.Buffered(1))


def _mod_spec(mod_rows, tm, d, rows_per_mod):
    if mod_rows == 1:
        tiles_per_mod = rows_per_mod // tm
        return pl.BlockSpec((1, 1, d), lambda i: (i // tiles_per_mod, 0, 0))
    return pl.BlockSpec((1, tm, d), lambda i: (0, i, 0))


def _sigmoid(x):
    return 1.0 / (1.0 + jnp.exp(-x))


def _silu(x):
    return x * _sigmoid(x)


def _rms(x, width):
    return lax.rsqrt(jnp.sum(x * x, axis=-1, keepdims=True) * (1.0 / width) + NORM_EPS)


def _dot(a, b):
    return jnp.dot(a, b, preferred_element_type=f32)


def _dot_nt(a, b):
    return lax.dot_general(a, b, (((1,), (1,)), ((), ())), preferred_element_type=f32)


def _floor_div(x, n):
    if n & (n - 1) == 0:
        return jnp.right_shift(x, n.bit_length() - 1)
    return x // n


def _split3(x):
    hi = x.astype(bf16)
    r = x - hi.astype(f32)
    mid = r.astype(bf16)
    lo = (r - mid.astype(f32)).astype(bf16)
    return hi, mid, lo


def _mod_kernel(c_ref, w_ref, b_ref, o_ref):
    a = _silu(c_ref[...]).astype(bf16)
    o_ref[...] = _dot(a, w_ref[...].astype(bf16)) + b_ref[...]


def _modulation(c, w_ada, b_ada):
    m, d = c.shape
    n = w_ada.shape[1]
    tn = min(MOD_TILE, n)
    return pl.pallas_call(
        _mod_kernel,
        grid=(n // tn,),
        in_specs=[pl.BlockSpec((m, d), lambda j: (0, 0)),
                  pl.BlockSpec((d, tn), lambda j: (0, j)),
                  pl.BlockSpec((1, tn), lambda j: (0, j))],
        out_specs=pl.BlockSpec((m, tn), lambda j: (0, j)),
        out_shape=jax.ShapeDtypeStruct((m, n), f32),
        compiler_params=_params(1, 40),
        name="modulation",
    )(c, w_ada, b_ada.reshape(1, n))


def _proj_kernel(x_ref, sh_ref, sc_ref, gn_ref, win_ref, gqa_ref, wqb_ref, gckv_ref, wuk_ref, wuv_ref,
                 small_ref, cos_ref, sin_ref,
                 ckv_ref, kpe_ref, kf_ref, vf_ref, logf_ref,
                 qm_ref, km_ref, vm_ref, qfb_ref, kfb_ref, vfb_ref, *, dims):
    d, qr, kr, n_mla, n_fox = dims
    fw = n_fox * HEAD
    x = x_ref[...]
    h = x * _rms(x, d) * gn_ref[...]
    h = h * (1.0 + sc_ref[0]) + sh_ref[0]
    hb = h.astype(bf16)
    cos = cos_ref[...]
    sin = sin_ref[...]
    g_qn, g_pea, g_peb = small_ref[0:1, :], small_ref[1:2, :], small_ref[2:3, :]
    g_ka, g_kb = small_ref[3:4, :], small_ref[4:5, :]
    g_qf, g_kf, b_f = small_ref[5:6, :], small_ref[6:7, :], small_ref[7:8, :]

    o_ckv = qr
    o_qf = o_ckv + kr
    o_kf = o_qf + fw
    o_vf = o_kf + fw
    o_ka = o_vf + fw
    o_kb = o_ka + LANES
    o_f = o_kb + LANES

    ka = _dot(hb, win_ref[:, o_ka:o_ka + LANES])
    kb = _dot(hb, win_ref[:, o_kb:o_kb + LANES])
    r = _rms(ka, ROPE)
    kpe = (ka * r * g_ka) * cos + (kb * r * g_kb) * sin
    kpe_ref[...] = kpe[:, :ROPE]
    kpe_b = kpe.astype(bf16)

    c = _dot(hb, win_ref[:, o_ckv:o_ckv + kr])
    cn = c * _rms(c, kr) * gckv_ref[...]
    ckv_ref[...] = cn
    cb = cn.astype(bf16)
    k_nope = _dot(cb, wuk_ref[...])
    vm_ref[...] = _dot(cb, wuv_ref[...]).astype(bf16)

    qa = _dot(hb, win_ref[:, 0:qr])
    qa = (qa * _rms(qa, qr) * gqa_ref[...]).astype(bf16)
    hw = n_mla * LANES
    for hd in range(n_mla):
        lo = hd * LANES
        qn = _dot(qa, wqb_ref[:, lo:lo + LANES])
        pa = _dot(qa, wqb_ref[:, hw + lo:hw + lo + LANES])
        pb = _dot(qa, wqb_ref[:, 2 * hw + lo:2 * hw + lo + LANES])
        ss = jnp.sum(qn * qn, axis=-1, keepdims=True) + jnp.sum(pa * pa, axis=-1, keepdims=True)
        r = lax.rsqrt(ss * (1.0 / (NOPE + ROPE)) + NORM_EPS)
        q_nope = qn * r * g_qn
        q_pe = (pa * r * g_pea) * cos + (pb * r * g_peb) * sin
        base = hd * MLA_QK
        qm_ref[:, base:base + LANES] = (q_nope * MLA_SCALE).astype(bf16)
        qm_ref[:, base + LANES:base + MLA_QK] = (q_pe * MLA_SCALE).astype(bf16)
        km_ref[:, base:base + LANES] = k_nope[:, lo:lo + LANES].astype(bf16)
        km_ref[:, base + LANES:base + MLA_QK] = kpe_b

    for hd in range(n_fox):
        lo = hd * HEAD
        qf = _dot(hb, win_ref[:, o_qf + lo:o_qf + lo + HEAD])
        qfb_ref[:, lo:lo + HEAD] = (qf * _rms(qf, HEAD) * g_qf * FOX_SCALE).astype(bf16)
        kf = _dot(hb, win_ref[:, o_kf + lo:o_kf + lo + HEAD])
        kf = kf * _rms(kf, HEAD) * g_kf
        kf_ref[:, lo:lo + HEAD] = kf
        kfb_ref[:, lo:lo + HEAD] = kf.astype(bf16)
    vf = _dot(hb, win_ref[:, o_vf:o_vf + fw])
    vf_ref[...] = vf
    vfb_ref[...] = vf.astype(bf16)
    z = _dot(hb, win_ref[:, o_f:o_f + LANES]) + b_f
    logf = jnp.minimum(z, 0.0) - jnp.log1p(jnp.exp(-jnp.abs(z)))
    logf_ref[...] = logf[:, :n_fox]


def _project(x, shift, scale, g_norm, pw, cos_t, sin_t, rows_per_mod, rows_per_pos):
    t, d = x.shape
    qr, kr, n_mla, n_fox = pw["dims"]
    fw = n_fox * HEAD
    tm = min(ROW_TILE, t)
    tiles_per_pos = max(rows_per_pos // tm, 1)
    pos_rows = cos_t.shape[0]
    tp = min(tm, pos_rows)
    mod_spec = _mod_spec(shift.shape[1], tm, d, rows_per_mod)
    pos_spec = pl.BlockSpec((tp, LANES), lambda i: (i % tiles_per_pos, 0))

    def row_spec(w):
        return pl.BlockSpec((tm, w), lambda i: (i, 0))

    out_widths = [(kr, f32), (ROPE, f32), (fw, f32), (fw, f32), (n_fox, f32),
                  (n_mla * MLA_QK, bf16), (n_mla * MLA_QK, bf16), (n_mla * HEAD, bf16),
                  (fw, bf16), (fw, bf16), (fw, bf16)]
    return pl.pallas_call(
        functools.partial(_proj_kernel, dims=(d, qr, kr, n_mla, n_fox)),
        grid=(t // tm,),
        in_specs=[row_spec(d), mod_spec, mod_spec, _const_spec((1, d)),
                  _const_spec(pw["w_in"].shape), _const_spec((1, qr)), _const_spec(pw["w_qb"].shape),
                  _const_spec((1, kr)), _const_spec(pw["w_uk"].shape), _const_spec(pw["w_uv"].shape),
                  _const_spec((8, LANES)), pos_spec, pos_spec],
        out_specs=[row_spec(w) for w, _ in out_widths],
        out_shape=[jax.ShapeDtypeStruct((t, w), dt) for w, dt in out_widths],
        compiler_params=_params(1, 56),
        name="project",
    )(x, shift, scale, g_norm, pw["w_in"], pw["g_qa"], pw["w_qb"], pw["g_ckv"], pw["w_uk"], pw["w_uv"],
      pw["small"], cos_t, sin_t)


def _pad_lanes(a):
    return jnp.pad(a, [(0, 0)] * (a.ndim - 1) + [(0, LANES - a.shape[-1])])


def _rot_half(a):
    half = a.shape[-1] // 2
    return jnp.concatenate([a[..., half:], a[..., :half]], axis=-1)


def _pack_proj_weights(w_in, g_q_a, w_q_b, g_q_mla, g_ckv, g_kpe, w_uk, w_uv, g_q_fox, g_k_fox, b_forget):
    qr = g_q_a.shape[0]
    kr = g_ckv.shape[0]
    n_mla = w_uk.shape[1]
    n_fox = b_forget.shape[0]
    fw = n_fox * HEAD
    o = 0
    w_qa = w_in[:, o:o + qr]; o += qr
    w_ckv = w_in[:, o:o + kr]; o += kr
    w_kpe = w_in[:, o:o + ROPE]; o += ROPE
    w_fox = w_in[:, o:o + 3 * fw]; o += 3 * fw
    w_f = w_in[:, o:o + n_fox]
    w_in_p = jnp.concatenate([w_qa, w_ckv, w_fox, _pad_lanes(w_kpe), _pad_lanes(_rot_half(w_kpe)),
                              _pad_lanes(w_f)], axis=1).astype(bf16)
    wq = w_q_b.reshape(qr, n_mla, NOPE + ROPE)
    w_nope = wq[:, :, :NOPE].reshape(qr, n_mla * NOPE)
    w_pe = wq[:, :, NOPE:]
    w_pea = _pad_lanes(w_pe).reshape(qr, n_mla * LANES)
    w_peb = _pad_lanes(_rot_half(w_pe)).reshape(qr, n_mla * LANES)
    w_qb_p = jnp.concatenate([w_nope, w_pea, w_peb], axis=1).astype(bf16)
    g_pe = g_q_mla[NOPE:]
    small = jnp.stack([g_q_mla[:NOPE], _pad_lanes(g_pe), _pad_lanes(_rot_half(g_pe)),
                       _pad_lanes(g_kpe), _pad_lanes(_rot_half(g_kpe)),
                       g_q_fox, g_k_fox, _pad_lanes(b_forget)]).astype(f32)
    return {"dims": (qr, kr, n_mla, n_fox), "w_in": w_in_p, "g_qa": g_q_a.reshape(1, qr), "w_qb": w_qb_p,
            "g_ckv": g_ckv.reshape(1, kr), "w_uk": w_uk.reshape(kr, n_mla * NOPE).astype(bf16),
            "w_uv": w_uv.reshape(kr, n_mla * HEAD).astype(bf16), "small": small}


def _rope_tables(pos):
    half = ROPE // 2
    inv = ROPE_THETA ** (-jnp.arange(half, dtype=f32) / half)
    ang = pos.astype(f32)[:, None] * inv[None, :]
    cos, sin = jnp.cos(ang), jnp.sin(ang)
    return _pad_lanes(jnp.concatenate([cos, cos], axis=1)), _pad_lanes(jnp.concatenate([-sin, sin], axis=1))


def _cumsum_kernel(x_ref, o_ref):
    s = x_ref.shape[1]
    x = x_ref[0]
    tri = (lax.broadcasted_iota(i32, (s, s), 1) <= lax.broadcasted_iota(i32, (s, s), 0)).astype(bf16)
    hi, mid, lo = _split3(x)
    o_ref[0] = _dot(tri, hi) + _dot(tri, mid) + _dot(tri, lo)


def _cumsum_seq(x):
    b, s, h = x.shape
    return pl.pallas_call(
        _cumsum_kernel,
        grid=(b,),
        in_specs=[pl.BlockSpec((1, s, h), lambda i: (i, 0, 0))],
        out_specs=pl.BlockSpec((1, s, h), lambda i: (i, 0, 0)),
        out_shape=jax.ShapeDtypeStruct((b, s, h), f32),
        compiler_params=_params(1, 40),
        name="cumsum_logf",
    )(x)


def _flash_kernel(*refs, has_bias, dk, dv):
    if has_bias:
        q_ref, k_ref, v_ref, cq_ref, ck_ref, o_ref = refs
    else:
        q_ref, k_ref, v_ref, o_ref = refs
    tq = q_ref.shape[0]
    n_heads = q_ref.shape[1] // dk
    i = pl.program_id(2)

    def step(j, carry, diagonal):
        start = pl.multiple_of(j * tq, tq)
        if diagonal:
            keep = lax.broadcasted_iota(i32, (tq, tq), 1) <= lax.broadcasted_iota(i32, (tq, tq), 0)
        out = []
        for hd in range(n_heads):
            m, l, acc = carry[hd]
            s = _dot_nt(q_ref[:, hd * dk:(hd + 1) * dk], k_ref[pl.ds(start, tq), hd * dk:(hd + 1) * dk])
            if has_bias:
                s = s + (cq_ref[0, hd] - ck_ref[0, hd, pl.ds(j, 1), :])
            if diagonal:
                s = jnp.where(keep, s, NEG_INF)
            m_new = jnp.maximum(m, jnp.max(s, axis=-1, keepdims=True))
            p = jnp.exp(s - m_new)
            corr = jnp.exp(m - m_new)
            l = l * corr + jnp.sum(p, axis=-1, keepdims=True)
            acc = acc * corr + _dot(p.astype(bf16), v_ref[pl.ds(start, tq), hd * dv:(hd + 1) * dv])
            out.append((m_new, l, acc))
        return tuple(out)

    init = tuple((jnp.full((tq, 1), NEG_INF, f32), jnp.zeros((tq, 1), f32), jnp.zeros((tq, dv), f32))
                 for _ in range(n_heads))
    carry = step(i, init, True)
    carry = lax.fori_loop(0, i, lambda j, c: step(j, c, False), carry)
    for hd in range(n_heads):
        m, l, acc = carry[hd]
        o_ref[:, hd * dv:(hd + 1) * dv] = (acc / l).astype(o_ref.dtype)


def _flash(q, k, v, batch, seq, n_heads, dk, dv, cum=None):
    t = q.shape[0]
    tq = min(ATT_TILE, seq)
    nq = seq // tq
    hb = ATT_HEADS if n_heads % ATT_HEADS == 0 else 1
    in_specs = [pl.BlockSpec((tq, hb * dk), lambda b, h, i: (b * nq + i, h)),
                pl.BlockSpec((seq, hb * dk), lambda b, h, i: (b, h)),
                pl.BlockSpec((seq, hb * dv), lambda b, h, i: (b, h))]
    args = [q, k, v]
    if cum is not None:
        cum_t = cum.transpose(0, 2, 1)
        in_specs += [pl.BlockSpec((1, hb, tq, 1), lambda b, h, i: (b, h, i, 0)),
                     pl.BlockSpec((1, hb, nq, tq), lambda b, h, i: (b, h, 0, 0))]
        args += [cum_t.reshape(batch, n_heads, seq, 1), cum_t.reshape(batch, n_heads, nq, tq)]
    return pl.pallas_call(
        functools.partial(_flash_kernel, has_bias=cum is not None, dk=dk, dv=dv),
        grid=(batch, n_heads // hb, nq),
        in_specs=in_specs,
        out_specs=pl.BlockSpec((tq, hb * dv), lambda b, h, i: (b * nq + i, h)),
        out_shape=jax.ShapeDtypeStruct((t, n_heads * dv), bf16),
        compiler_params=_params(3, 40),
        name="flash_bias" if cum is not None else "flash",
    )(*args)


def _absorb_kernel(q_ref, wuk_ref, o_ref, *, n_mla):
    kr = wuk_ref.shape[0]
    for hd in range(n_mla):
        qn = q_ref[:, hd * MLA_QK:hd * MLA_QK + NOPE]
        o_ref[:, hd * kr:(hd + 1) * kr] = _dot_nt(qn, wuk_ref[:, hd * NOPE:(hd + 1) * NOPE]).astype(bf16)


def _absorb(qm, w_uk2, n_mla):
    bd = qm.shape[0]
    kr = w_uk2.shape[0]
    return pl.pallas_call(
        functools.partial(_absorb_kernel, n_mla=n_mla),
        out_shape=jax.ShapeDtypeStruct((bd, n_mla * kr), bf16),
        name="absorb_query",
    )(qm, w_uk2)


def _expand_kernel(olat_ref, ofox_ref, wuv_ref, mla_ref, fox_ref, *, n_mla):
    kr = wuv_ref.shape[0]
    for hd in range(n_mla):
        o = olat_ref[:, hd * kr:(hd + 1) * kr].astype(bf16)
        mla_ref[:, hd * HEAD:(hd + 1) * HEAD] = _dot(o, wuv_ref[:, hd * HEAD:(hd + 1) * HEAD]).astype(bf16)
    fox_ref[...] = ofox_ref[...].astype(bf16)


def _expand(o_lat, o_fox, w_uv2, n_mla):
    bd = o_lat.shape[0]
    return pl.pallas_call(
        functools.partial(_expand_kernel, n_mla=n_mla),
        out_shape=[jax.ShapeDtypeStruct((bd, n_mla * HEAD), bf16),
                   jax.ShapeDtypeStruct(o_fox.shape, bf16)],
        name="expand_latent",
    )(o_lat, o_fox, w_uv2)


def _decode_kernel(pt_ref, qlat_ref, qpe_ref, qf_ref, fnew_ref, ckvn_ref, kpen_ref, kfn_ref, vfn_ref,
                   ckv_hbm, kpe_hbm, fk_hbm, fv_hbm, lf_hbm, olat_ref, ofox_ref,
                   ckv_buf, kpe_buf, fk_buf, fv_buf, lf_buf, sems, m_m, l_m, acc_m, m_f, l_f, acc_f, suf,
                   *, n_pages_step, n_fox):
    g = n_pages_step
    b = pl.program_id(0)
    j = pl.program_id(1)
    steps = pl.num_programs(1)
    n_pages = steps * g
    step = b * steps + j
    slot = lax.rem(step, 2)
    page = ckv_buf.shape[2]
    rows = page * n_fox

    def page_copies(bb, jj, sl):
        copies = []
        for p in range(g):
            pid = pt_ref[bb, n_pages - 1 - (jj * g + p)]
            for hbm, buf in ((ckv_hbm, ckv_buf), (kpe_hbm, kpe_buf), (fk_hbm, fk_buf), (fv_hbm, fv_buf),
                             (lf_hbm, lf_buf)):
                copies.append(pltpu.make_async_copy(hbm.at[pid], buf.at[sl, p], sems.at[sl]))
        return copies

    @pl.when(step == 0)
    def _():
        for c in page_copies(0, 0, 0):
            c.start()

    @pl.when(step + 1 < pl.num_programs(0) * steps)
    def _():
        wrap = j + 1 == steps
        for c in page_copies(jnp.where(wrap, b + 1, b), jnp.where(wrap, 0, j + 1), 1 - slot):
            c.start()

    for c in page_copies(b, j, slot):
        c.wait()
    ckv_refs = [ckv_buf.at[slot, p] for p in range(g)]
    kpe_refs = [kpe_buf.at[slot, p] for p in range(g)]
    fk_refs = [fk_buf.at[slot, p] for p in range(g)]
    fv_refs = [fv_buf.at[slot, p] for p in range(g)]
    lf_refs = [lf_buf.at[slot, p] for p in range(g)]

    q_lat = qlat_ref[0]
    q_pe = qpe_ref[0][:, :ROPE]
    q_fox = qf_ref[0]
    f_new = fnew_ref[0]

    @pl.when(j == 0)
    def _():
        ckv_n = ckvn_ref[0].astype(bf16).astype(f32)
        kpe_n = kpen_ref[0].astype(bf16).astype(f32)
        s_m = (jnp.sum(q_lat.astype(f32) * ckv_n, axis=-1, keepdims=True)
               + jnp.sum(q_pe.astype(f32) * kpe_n, axis=-1, keepdims=True))
        m_m[...] = s_m
        l_m[...] = jnp.ones_like(s_m)
        acc_m[...] = jnp.broadcast_to(ckv_n, acc_m.shape)
        kf_n = kfn_ref[0].astype(bf16).astype(f32)
        s_f = jnp.sum(q_fox.astype(f32) * kf_n, axis=-1, keepdims=True)
        m_f[...] = s_f
        l_f[...] = jnp.ones_like(s_f)
        acc_f[...] = vfn_ref[0].astype(bf16).astype(f32)
        suf[...] = jnp.zeros_like(suf)

    col = lax.broadcasted_iota(i32, (n_fox, rows), 1)
    own = (col - _floor_div(col, n_fox) * n_fox) == lax.broadcasted_iota(i32, (n_fox, rows), 0)
    later = (lax.broadcasted_iota(i32, (page, rows), 0)
             > _floor_div(lax.broadcasted_iota(i32, (page, rows), 1), n_fox)).astype(bf16)
    suffix = suf[...]
    s_mla, s_fox, ckv_b, fv_b = [], [], [], []
    for p in range(g):
        cb = ckv_refs[p][...].astype(bf16)
        ckv_b.append(cb)
        s_mla.append(_dot_nt(q_lat, cb) + _dot(q_pe, kpe_refs[p][...].astype(bf16)))
        lf = lf_refs[p][...]
        hi, mid, lo = _split3(lf)
        parts = jnp.concatenate([hi.astype(f32), mid.astype(f32), lo.astype(f32)], axis=0).astype(bf16)
        sums = _dot(parts, later)
        after = sums[0:n_fox] + sums[n_fox:2 * n_fox] + sums[2 * n_fox:3 * n_fox]
        s = _dot_nt(q_fox, fk_refs[p][...].astype(bf16)) + (after + (suffix + f_new))
        s_fox.append(jnp.where(own, s, NEG_INF))
        suffix = suffix + jnp.sum(lf, axis=-1, keepdims=True)
        fv_b.append(fv_refs[p][...].astype(bf16))
    suf[...] = suffix

    def update(s_list, vals, m_ref, l_ref, acc_ref):
        width = s_list[0].shape[1]
        s = jnp.concatenate(s_list, axis=1)
        m_old = m_ref[...]
        m_new = jnp.maximum(m_old, jnp.max(s, axis=-1, keepdims=True))
        corr = jnp.exp(m_old - m_new)
        pr = jnp.exp(s - m_new)
        l_ref[...] = l_ref[...] * corr + jnp.sum(pr, axis=-1, keepdims=True)
        pb = pr.astype(bf16)
        acc = acc_ref[...] * corr
        for p in range(g):
            acc = acc + _dot(pb[:, p * width:(p + 1) * width], vals[p])
        acc_ref[...] = acc
        m_ref[...] = m_new

    update(s_mla, ckv_b, m_m, l_m, acc_m)
    update(s_fox, fv_b, m_f, l_f, acc_f)

    @pl.when(j == pl.num_programs(1) - 1)
    def _():
        olat_ref[0] = acc_m[...] / l_m[...]
        ofox_ref[0] = acc_f[...] / l_f[...]


def _decode(page_table, q_lat, q_pe, q_fox, f_new, ckv_new, kpe_new, kf_new, vf_new,
            cache_ckv, cache_kpe, cache_fk, cache_fv, cache_logf):
    bd, n_pages = page_table.shape
    n_mla, kr = q_lat.shape[1:]
    n_fox = q_fox.shape[1]
    page = cache_ckv.shape[1]
    g = min(DECODE_PAGES, n_pages)
    steps = n_pages // g

    def per_batch(shape):
        nd = len(shape)
        return pl.BlockSpec((1,) + tuple(shape[1:]), lambda b, j, pt: (b,) + (0,) * (nd - 1))

    small_in = [q_lat, q_pe, q_fox, f_new, ckv_new, kpe_new, kf_new, vf_new]
    caches = [cache_ckv, cache_kpe, cache_fk, cache_fv, cache_logf]
    in_specs = [per_batch(a.shape) for a in small_in] + [pl.BlockSpec(memory_space=pl.ANY)] * len(caches)
    page_bufs = [pltpu.VMEM((2, g) + tuple(arr.shape[1:]), f32) for arr in caches]
    grid_spec = pltpu.PrefetchScalarGridSpec(
        num_scalar_prefetch=1,
        grid=(bd, steps),
        in_specs=in_specs,
        out_specs=[pl.BlockSpec((1, n_mla, kr), lambda b, j, pt: (b, 0, 0)),
                   pl.BlockSpec((1, n_fox, HEAD), lambda b, j, pt: (b, 0, 0))],
        scratch_shapes=page_bufs + [
            pltpu.SemaphoreType.DMA((2,)),
            pltpu.VMEM((n_mla, 1), f32), pltpu.VMEM((n_mla, 1), f32), pltpu.VMEM((n_mla, kr), f32),
            pltpu.VMEM((n_fox, 1), f32), pltpu.VMEM((n_fox, 1), f32), pltpu.VMEM((n_fox, HEAD), f32),
            pltpu.VMEM((n_fox, 1), f32)],
    )
    return pl.pallas_call(
        functools.partial(_decode_kernel, n_pages_step=g, n_fox=n_fox),
        grid_spec=grid_spec,
        out_shape=[jax.ShapeDtypeStruct((bd, n_mla, kr), f32), jax.ShapeDtypeStruct((bd, n_fox, HEAD), f32)],
        compiler_params=_params(2, 48),
        name="paged_decode",
    )(page_table, *small_in, *caches)


def _pack_pairs(h):
    half = h.shape[1] // 2
    hb = h.astype(bf16).astype(f32)
    top = pltpu.bitcast(hb[:, :half], u32)
    bot = pltpu.bitcast(hb[:, half:], u32)
    return top | (bot >> 16)


def _unpack_pairs(w):
    top = pltpu.bitcast(w & jnp.uint32(0xFFFF0000), f32)
    bot = pltpu.bitcast(w << 16, f32)
    return top, bot


def _unpack_bf16(w):
    top, bot = _unpack_pairs(w)
    return jnp.concatenate([top, bot], axis=1).astype(bf16)


def _oproj_kernel(mla_ref, fox_ref, woa_ref, wob_ref, x_ref, gt_ref, sh_ref, sc_ref, gn_ref, wr_ref, br_ref,
                  cin_ref, x1_ref, hp_ref, idx_ref, wt_ref, rank_ref, cnt_ref, carry_ref):
    d = x_ref.shape[1]
    tm = x_ref.shape[0]
    n_exp = wr_ref.shape[0]
    per_group = n_exp // N_GROUPS

    @pl.when(pl.program_id(0) == 0)
    def _():
        carry_ref[...] = cin_ref[...]

    mix = _dot(mla_ref[...], woa_ref[...]) + _dot(fox_ref[...], wob_ref[...])
    x1 = x_ref[...] + gt_ref[0] * mix
    x1_ref[...] = x1
    h = x1 * _rms(x1, d) * gn_ref[...]
    h = h * (1.0 + sc_ref[0]) + sh_ref[0]
    hp_ref[...] = _pack_pairs(h)

    h_hi = h.astype(bf16)
    h_lo = (h - h_hi.astype(f32)).astype(bf16)
    wr = wr_ref[...]
    w_hi = wr.astype(bf16)
    w_lo = (wr - w_hi.astype(f32)).astype(bf16)
    logits = _dot_nt(w_hi, h_hi) + (_dot_nt(w_hi, h_lo) + _dot_nt(w_lo, h_hi))
    scores = _sigmoid(logits).reshape(N_GROUPS, per_group, tm)
    sel = scores + br_ref[...].reshape(N_GROUPS, per_group, 1)

    in_group = lax.broadcasted_iota(i32, (N_GROUPS, per_group, tm), 1).astype(f32)
    group_id = lax.broadcasted_iota(i32, (N_GROUPS, per_group, tm), 0).astype(f32)
    expert_id = group_id * per_group + in_group

    m1 = jnp.max(sel, axis=1, keepdims=True)
    first = jnp.min(jnp.where(sel == m1, in_group, float(per_group)), axis=1, keepdims=True)
    m2 = jnp.max(jnp.where(in_group == first, NEG_INF, sel), axis=1, keepdims=True)
    gscore = m1 + m2
    gid = lax.broadcasted_iota(i32, (N_GROUPS, 1, tm), 0).astype(f32)
    gkeep = jnp.zeros((N_GROUPS, 1, tm), f32)
    for _ in range(TOPK_GROUPS):
        best = jnp.max(gscore, axis=0, keepdims=True)
        pick = jnp.min(jnp.where(gscore == best, gid, float(N_GROUPS)), axis=0, keepdims=True)
        hit = gid == pick
        gkeep = jnp.where(hit, 1.0, gkeep)
        gscore = jnp.where(hit, NEG_INF, gscore)

    cand = jnp.where(jnp.broadcast_to(gkeep, sel.shape) > 0.5, sel, NEG_INF)
    chosen = jnp.zeros((N_GROUPS, per_group, tm), f32)
    picks, weights = [], []
    for _ in range(TOP_K):
        best = jnp.max(jnp.max(cand, axis=1, keepdims=True), axis=0, keepdims=True)
        pick = jnp.min(jnp.min(jnp.where(cand == best, expert_id, float(n_exp)), axis=1, keepdims=True),
                       axis=0, keepdims=True)
        hit = expert_id == pick
        weights.append(jnp.sum(jnp.sum(jnp.where(hit, scores, 0.0), axis=1, keepdims=True), axis=0))
        picks.append(pick)
        chosen = jnp.where(hit, 1.0, chosen)
        cand = jnp.where(hit, NEG_INF, cand)

    wsum = weights[0]
    for w in weights[1:]:
        wsum = wsum + w
    chosen2 = chosen.reshape(n_exp, tm)
    upto = (lax.broadcasted_iota(i32, (tm, tm), 0) <= lax.broadcasted_iota(i32, (tm, tm), 1)).astype(bf16)
    incl = _dot(chosen2.astype(bf16), upto)
    before = (carry_ref[...] + (incl - chosen2)).reshape(N_GROUPS, per_group, tm)
    carry_ref[...] = carry_ref[...] + jnp.sum(chosen2, axis=1, keepdims=True)
    cnt_ref[...] = carry_ref[...]
    for k in range(TOP_K):
        hit = expert_id == picks[k]
        rk = jnp.sum(jnp.sum(jnp.where(hit, before, 0.0), axis=1, keepdims=True), axis=0)
        idx_ref[k:k + 1, :] = picks[k][0].astype(i32)
        wt_ref[k:k + 1, :] = weights[k] / wsum * ROUTED_SCALE
        rank_ref[k:k + 1, :] = rk.astype(i32)


def _oproj_router(mla, fox, w_o_a, w_o_b, x, gate, shift, scale, g_norm, w_router_t, b_router, counts_in,
                  rows_per_mod):
    t, d = x.shape
    tm = min(ROW_TILE, t)
    n_exp = w_router_t.shape[0]
    mod_spec = _mod_spec(shift.shape[1], tm, d, rows_per_mod)

    def row_spec(w):
        return pl.BlockSpec((tm, w), lambda i: (i, 0))

    def tok_spec():
        return pl.BlockSpec((TOP_K, tm), lambda i: (0, i))

    return pl.pallas_call(
        _oproj_kernel,
        grid=(t // tm,),
        in_specs=[row_spec(mla.shape[1]), row_spec(fox.shape[1]), _const_spec(w_o_a.shape), _const_spec(w_o_b.shape),
                  row_spec(d), mod_spec, mod_spec, mod_spec, _const_spec((1, d)),
                  _const_spec((n_exp, d)), _const_spec((n_exp, 1)), _const_spec((n_exp, 1))],
        out_specs=[row_spec(d), row_spec(d // 2), tok_spec(), tok_spec(), tok_spec(),
                   pl.BlockSpec((n_exp, 1), lambda i: (0, 0))],
        out_shape=[jax.ShapeDtypeStruct((t, d), f32), jax.ShapeDtypeStruct((t, d // 2), u32),
                   jax.ShapeDtypeStruct((TOP_K, t), i32), jax.ShapeDtypeStruct((TOP_K, t), f32),
                   jax.ShapeDtypeStruct((TOP_K, t), i32), jax.ShapeDtypeStruct((n_exp, 1), f32)],
        scratch_shapes=[pltpu.VMEM((n_exp, 1), f32)],
        compiler_params=_params(1, 48),
        name="oproj_router",
    )(mla, fox, w_o_a, w_o_b, x, gate, shift, scale, g_norm, w_router_t, b_router, counts_in)


def _dispatch_kernel(zf_ref, dest_ref, hp_ref, *rest, creates):
    if creates:
        xs_ref, zeros_ref, sem, zsem = rest
    else:
        _, xs_ref, zeros_ref, sem, zsem = rest
    tm = hp_ref.shape[0]
    blk = zeros_ref.shape[0]

    if creates:
        @pl.when(pl.program_id(0) == 0)
        def _():
            zeros_ref[...] = jnp.zeros_like(zeros_ref)

            def zero_copy(i):
                return pltpu.make_async_copy(zeros_ref, xs_ref.at[pl.ds(pl.multiple_of(i * blk, blk), blk)], zsem)

            def issue_zero(i, c):
                @pl.when(zf_ref[i] > 0)
                def _():
                    zero_copy(i).start()
                return c

            def drain_zero(i, c):
                @pl.when(zf_ref[i] > 0)
                def _():
                    zero_copy(i).wait()
                return c

            lax.fori_loop(0, zf_ref.shape[0], issue_zero, 0)
            lax.fori_loop(0, zf_ref.shape[0], drain_zero, 0)

    def row_copy(r, k):
        return pltpu.make_async_copy(hp_ref.at[pl.ds(r, 1)], xs_ref.at[pl.ds(dest_ref[k, r], 1)], sem)

    def issue(r, c):
        for k in range(TOP_K):
            row_copy(r, k).start()
        return c

    lax.fori_loop(0, tm, issue, 0)

    def drain(r, c):
        for k in range(TOP_K):
            row_copy(r, k).wait()
        return c

    lax.fori_loop(0, tm, drain, 0)


def _dispatch(zero_flag, dest, hp, xs=None):
    t, half = hp.shape
    tm = min(ROW_TILE, t)
    creates = xs is None
    n_rows = zero_flag.shape[0] * EXPERT_BLOCK
    in_specs = [pl.BlockSpec((TOP_K, tm), lambda i, zf: (0, i), memory_space=pltpu.SMEM),
                pl.BlockSpec((tm, half), lambda i, zf: (i, 0))]
    if not creates:
        in_specs.append(pl.BlockSpec(memory_space=pl.ANY))
    grid_spec = pltpu.PrefetchScalarGridSpec(
        num_scalar_prefetch=1,
        grid=(t // tm,),
        in_specs=in_specs,
        out_specs=pl.BlockSpec(memory_space=pl.ANY),
        scratch_shapes=[pltpu.VMEM((EXPERT_BLOCK, half), u32), pltpu.SemaphoreType.DMA(()),
                        pltpu.SemaphoreType.DMA(())],
    )
    return pl.pallas_call(
        functools.partial(_dispatch_kernel, creates=creates),
        grid_spec=grid_spec,
        out_shape=jax.ShapeDtypeStruct((n_rows, half), u32),
        input_output_aliases={} if creates else {3: 0},
        compiler_params=_params(1, 32),
        name="dispatch_rows",
    )(zero_flag, dest, hp, *(() if creates else (xs,)))


def _expert_kernel(be_ref, nx_ref, gp_ref, nu_ref, xs_ref, wg_hbm, wu_hbm, wd_hbm, o_ref,
                   wgf, wuf, wdf, wgb, wub, wdb, sems):
    i = pl.program_id(0)

    def weight_copies(e, sl):
        return [pltpu.make_async_copy(hbm.at[e], buf.at[sl], sems.at[sl])
                for hbm, buf in ((wg_hbm, wgf), (wu_hbm, wuf), (wd_hbm, wdf))]

    @pl.when(i == 0)
    def _():
        for c in weight_copies(be_ref[0], 0):
            c.start()

    @pl.when(jnp.logical_and(nx_ref[i] > -2, i < nu_ref[0]))
    def _():
        sl = gp_ref[i]
        for c in weight_copies(be_ref[i], sl):
            c.wait()

        @pl.when(nx_ref[i] >= 0)
        def _():
            for c in weight_copies(nx_ref[i], 1 - sl):
                c.start()

        wgb[...] = wgf[sl].astype(bf16)
        wub[...] = wuf[sl].astype(bf16)
        wdb[...] = wdf[sl].astype(bf16)

    @pl.when(i < nu_ref[0])
    def _():
        x = _unpack_bf16(xs_ref[...])
        a = (_silu(_dot(x, wgb[...])) * _dot(x, wub[...])).astype(bf16)
        o_ref[...] = _pack_pairs(_dot(a, wdb[...]))

    @pl.when(i >= nu_ref[0])
    def _():
        o_ref[...] = jnp.zeros_like(o_ref)


def _experts(block_expert, next_expert, group_parity, n_used, xs, wg, wu, wd):
    n_rows, half = xs.shape
    d = 2 * half
    ff = wg.shape[2]
    blk = EXPERT_BLOCK
    n_blocks = n_rows // blk

    def row_map(i, be, nx, gp, nu):
        return (jnp.minimum(i, nu[0] - 1), 0)

    grid_spec = pltpu.PrefetchScalarGridSpec(
        num_scalar_prefetch=4,
        grid=(n_blocks,),
        in_specs=[pl.BlockSpec((blk, half), row_map)] + [pl.BlockSpec(memory_space=pl.ANY)] * 3,
        out_specs=pl.BlockSpec((blk, half), lambda i, be, nx, gp, nu: (i, 0)),
        scratch_shapes=[pltpu.VMEM((2, d, ff), f32), pltpu.VMEM((2, d, ff), f32), pltpu.VMEM((2, ff, d), f32),
                        pltpu.VMEM((d, ff), bf16), pltpu.VMEM((d, ff), bf16), pltpu.VMEM((ff, d), bf16),
                        pltpu.SemaphoreType.DMA((2,))],
    )
    return pl.pallas_call(
        _expert_kernel,
        grid_spec=grid_spec,
        out_shape=jax.ShapeDtypeStruct((n_rows, half), u32),
        compiler_params=_params(1, 52),
        name="expert_mlp",
    )(block_expert, next_expert, group_parity, n_used, xs, wg, wu, wd)


def _combine_kernel(dest_ref, hp_ref, x1_ref, gt_ref, wt_ref, eo_ref, sg_ref, su_ref, sd_ref, y_ref, rows, sem):
    tm, half = hp_ref.shape

    def row_copy(r, k):
        return pltpu.make_async_copy(eo_ref.at[pl.ds(dest_ref[k, r], 1)], rows.at[k, pl.ds(r, 1)], sem)

    def issue(part):
        for r in range(part * tm // 4, (part + 1) * tm // 4):
            for k in range(TOP_K):
                row_copy(r, k).start()

    issue(0)
    x = _unpack_bf16(hp_ref[...])
    issue(1)
    gate_act = _silu(_dot(x, sg_ref[...]))
    issue(2)
    a = (gate_act * _dot(x, su_ref[...])).astype(bf16)
    issue(3)
    y = _dot(a, sd_ref[...])
    y_top, y_bot = y[:, :half], y[:, half:]

    def drain(r, c):
        for k in range(TOP_K):
            row_copy(r, k).wait()
        return c

    lax.fori_loop(0, tm, drain, 0)

    wt = wt_ref[...]
    for k in range(TOP_K):
        top, bot = _unpack_pairs(rows[k])
        y_top = y_top + wt[:, k:k + 1] * top
        y_bot = y_bot + wt[:, k:k + 1] * bot
    gate = gt_ref[0]
    y_ref[:, :half] = x1_ref[:, :half] + gate[:, :half] * y_top
    y_ref[:, half:] = x1_ref[:, half:] + gate[:, half:] * y_bot


def _combine(dest, hp, x1, gate, wt_rows, eo, sg, su, sd, rows_per_mod):
    t, d = x1.shape
    tm = min(COMBINE_TILE, t)
    mod_spec = _mod_spec(gate.shape[1], tm, d, rows_per_mod)
    return pl.pallas_call(
        _combine_kernel,
        grid=(t // tm,),
        in_specs=[pl.BlockSpec((TOP_K, tm), lambda i: (0, i), memory_space=pltpu.SMEM),
                  pl.BlockSpec((tm, d // 2), lambda i: (i, 0)),
                  pl.BlockSpec((tm, d), lambda i: (i, 0)),
                  mod_spec,
                  pl.BlockSpec((tm, TOP_K), lambda i: (i, 0)),
                  pl.BlockSpec(memory_space=pl.ANY),
                  _const_spec(sg.shape), _const_spec(su.shape), _const_spec(sd.shape)],
        out_specs=pl.BlockSpec((tm, d), lambda i: (i, 0)),
        out_shape=jax.ShapeDtypeStruct((t, d), f32),
        scratch_shapes=[pltpu.VMEM((TOP_K, tm, d // 2), u32), pltpu.SemaphoreType.DMA(())],
        compiler_params=_params(1, 48),
        name="combine_shared",
    )(dest, hp, x1, gate, wt_rows, eo, sg, su, sd)


def _moe(groups, counts, wg, wu, wd, sg, su, sd, first_expert):
    n_exp = counts.shape[0]
    blk = EXPERT_BLOCK
    n_tok = sum(g["x1"].shape[0] for g in groups)
    n_blocks = -(-(n_tok * TOP_K) // blk) + n_exp
    cnt = counts[:, 0].astype(i32)
    padded = (cnt + blk - 1) // blk * blk
    pad_end = jnp.cumsum(padded)
    pad_start = pad_end - padded
    expert_ids = jnp.arange(n_exp, dtype=i32)
    block_rows = jnp.arange(n_blocks, dtype=i32) * blk
    local_expert = jnp.minimum(jnp.sum((pad_end[None, :] <= block_rows[:, None]).astype(i32), axis=1), n_exp - 1)
    block_expert = (local_expert + first_expert).astype(i32)
    n_used = (pad_end[-1:] // blk).astype(i32)
    nonempty = padded > 0
    parity_e = (jnp.cumsum(nonempty.astype(i32)) - 1) % 2
    later_ne = jnp.logical_and(nonempty[None, :], expert_ids[None, :] > expert_ids[:, None])
    next_e = jnp.min(jnp.where(later_ne, expert_ids[None, :], n_exp), axis=1)
    next_e = jnp.where(next_e < n_exp, next_e + first_expert, -1)
    onehot = (local_expert[:, None] == expert_ids[None, :]).astype(i32)
    is_first = jnp.sum(onehot * pad_start[None, :], axis=1) == block_rows
    is_first = jnp.logical_and(is_first, block_rows < pad_end[-1])
    next_expert = jnp.where(is_first, jnp.sum(onehot * next_e[None, :], axis=1), -2).astype(i32)
    group_parity = jnp.sum(onehot * parity_e[None, :], axis=1).astype(i32)

    first_end = pad_start + groups[0]["counts"][:, 0].astype(i32)
    covered = jnp.sum(onehot * first_end[None, :], axis=1) >= block_rows + blk
    zero_flag = jnp.logical_or(jnp.logical_not(covered), block_rows >= pad_end[-1]).astype(i32)

    xs = None
    dests = []
    for g in groups:
        start_of = jnp.sum(jnp.where(g["idx"][None] == expert_ids[:, None, None], pad_start[:, None, None], 0), axis=0)
        dest = start_of + g["rank"]
        dests.append(dest)
        xs = _dispatch(zero_flag, dest, g["hp"], xs)
    eo = _experts(block_expert, next_expert, group_parity, n_used, xs, wg, wu, wd)
    return [_combine(dest, g["hp"], g["x1"], g["gate"], g["wt"].T, eo, sg, su, sd, g["rows_per_mod"])
            for dest, g in zip(dests, groups)]


def kernel(x_prompt, x_sample, cache_mla_ckv, cache_mla_kpe, cache_fox_k, cache_fox_v, cache_fox_logf, page_table, c_prompt, c_sample, w_ada, b_ada, g_norm_attn, g_norm_moe, w_in, g_q_a, w_q_b, g_q_mla, g_ckv, g_kpe, w_uk, w_uv, g_q_fox, g_k_fox, b_forget, w_o, w_router, b_router, w_exp_gate, w_exp_up, w_exp_down, w_sh_gate, w_sh_up, w_sh_down):
    depth = w_in.shape[0]
    b, s, d = x_prompt.shape
    bd, ds = x_sample.shape[:2]
    assert ds == 1, "the sample group decodes one token per sequence"
    n_pool, page = cache_mla_ckv.shape[1:3]
    n_pages = page_table.shape[1]
    n_fox, hd_fox = cache_fox_k.shape[3:]
    assert hd_fox == HEAD
    fw = n_fox * HEAD
    kr = cache_mla_ckv.shape[3]
    n_mla = w_uk.shape[2]
    hw = n_mla * HEAD
    n_exp = w_router.shape[2]
    assert n_exp % N_GROUPS == 0 and n_exp // N_GROUPS == 8

    cos_p, sin_p = _rope_tables(jnp.arange(s))
    cos_s, sin_s = _rope_tables(jnp.full((1,), n_pages * page))

    xp = x_prompt.reshape(b * s, d)
    xs = x_sample.reshape(bd, d)
    ff = w_exp_gate.shape[3]
    wg_all = w_exp_gate.reshape(depth * n_exp, d, ff)
    wu_all = w_exp_up.reshape(depth * n_exp, d, ff)
    wd_all = w_exp_down.reshape(depth * n_exp, ff, d)
    ckv_all = cache_mla_ckv.reshape(depth * n_pool, page, kr)
    fk_all = cache_fox_k.reshape(depth * n_pool, page * n_fox, HEAD)
    fv_all = cache_fox_v.reshape(depth * n_pool, page * n_fox, HEAD)
    kpe_all = jnp.swapaxes(cache_mla_kpe, 2, 3).reshape(depth * n_pool, ROPE, page)
    lf_all = jnp.swapaxes(cache_fox_logf, 2, 3).reshape(depth * n_pool, n_fox, page)
    st = [[] for _ in range(10)]
    for l in range(depth):
        pw = _pack_proj_weights(w_in[l], g_q_a[l], w_q_b[l], g_q_mla[l], g_ckv[l], g_kpe[l], w_uk[l], w_uv[l],
                                g_q_fox[l], g_k_fox[l], b_forget[l])
        w_o_a = w_o[l][:hw].astype(bf16)
        w_o_b = w_o[l][hw:].astype(bf16)
        w_router_t = w_router[l].T
        b_router_c = b_router[l].reshape(n_exp, 1)
        g_attn = g_norm_attn[l].reshape(1, d)
        g_moe = g_norm_moe[l].reshape(1, d)
        sg, su, sd = w_sh_gate[l].astype(bf16), w_sh_up[l].astype(bf16), w_sh_down[l].astype(bf16)
        moe_w = (wg_all, wu_all, wd_all, sg, su, sd)

        mod = _modulation(jnp.concatenate([c_prompt, c_sample], axis=0), w_ada[l], b_ada[l])
        mod_p = [m.reshape(b, 1, d) for m in jnp.split(mod[:b], 6, axis=-1)]
        mod_s = [m.reshape(1, bd, d) for m in jnp.split(mod[b:], 6, axis=-1)]

        sh1, sc1, gt1, sh2, sc2, gt2 = mod_p
        (ckv, kpe, kf, vf, logf, qm, km, vm, qfb, kfb, vfb) = _project(
            xp, sh1, sc1, g_attn, pw, cos_p, sin_p, rows_per_mod=s, rows_per_pos=s)
        cum = _cumsum_seq(logf.reshape(b, s, n_fox))
        o_mla = _flash(qm, km, vm, b, s, n_mla, MLA_QK, HEAD)
        o_fox = _flash(qfb, kfb, vfb, b, s, n_fox, HEAD, HEAD, cum=cum)
        x1, hp, idx, wt, rank, counts = _oproj_router(o_mla, o_fox, w_o_a, w_o_b, xp, gt1, sh2, sc2, g_moe,
                                                      w_router_t, b_router_c, jnp.zeros((n_exp, 1), f32),
                                                      rows_per_mod=s)
        group_p = dict(hp=hp, idx=idx, wt=wt, rank=rank, x1=x1, gate=gt2, rows_per_mod=s, counts=counts)
        for lst, arr in zip(st[:5], (ckv.reshape(b, s, kr), kpe.reshape(b, s, ROPE), kf.reshape(b, s, n_fox, HEAD),
                                     vf.reshape(b, s, n_fox, HEAD), logf.reshape(b, s, n_fox))):
            lst.append(arr)

        sh1, sc1, gt1, sh2, sc2, gt2 = mod_s
        (ckv, kpe, kf, vf, logf, qm, km, vm, qfb, kfb, vfb) = _project(
            xs, sh1, sc1, g_attn, pw, cos_s, sin_s, rows_per_mod=bd, rows_per_pos=bd)
        q_lat = _absorb(qm, pw["w_uk"], n_mla).reshape(bd, n_mla, kr)
        q_pe = qm.reshape(bd, n_mla, MLA_QK)[:, :, NOPE:]
        o_lat, o_fox = _decode(page_table + l * n_pool, q_lat, q_pe, qfb.reshape(bd, n_fox, HEAD),
                               logf.reshape(bd, n_fox, 1), ckv.reshape(bd, 1, kr), kpe.reshape(bd, 1, ROPE),
                               kf.reshape(bd, n_fox, HEAD), vf.reshape(bd, n_fox, HEAD),
                               ckv_all, kpe_all, fk_all, fv_all, lf_all)
        mix_mla, mix_fox = _expand(o_lat.reshape(bd, n_mla * kr), o_fox.reshape(bd, fw), pw["w_uv"], n_mla)
        x1, hp, idx, wt, rank, counts = _oproj_router(mix_mla, mix_fox, w_o_a, w_o_b, xs, gt1, sh2, sc2, g_moe,
                                                      w_router_t, b_router_c, counts, rows_per_mod=bd)
        group_s = dict(hp=hp, idx=idx, wt=wt, rank=rank, x1=x1, gate=gt2, rows_per_mod=bd, counts=counts)
        xp, xs = _moe([group_p, group_s], counts, *moe_w, first_expert=l * n_exp)
        for lst, arr in zip(st[5:], (ckv.reshape(bd, 1, kr), kpe.reshape(bd, 1, ROPE), kf.reshape(bd, 1, n_fox, HEAD),
                                     vf.reshape(bd, 1, n_fox, HEAD), logf.reshape(bd, 1, n_fox))):
            lst.append(arr)

    stacked = [jnp.stack(a, 0) for a in st]
    return (xp.reshape(b, s, d), xs.reshape(bd, 1, d), *stacked)
```
